```python
import math
import jax
import jax.numpy as jnp
from jax import lax
import numpy as np

D_MODEL = 1024
BATCH = 8
SEQ = 2048
DEPTH = 4
DEC_BATCH = 32
DEC_SEQ = 1
PAST_LEN = 8192
PAGE_SIZE = 128

N_BRANCH = 4
BRANCH_W = D_MODEL // 2
FOX_HEADS = 8
FOX_KV_HEADS = 4
FOX_HD = BRANCH_W // FOX_HEADS
SB_HEADS = 8
SB_KV_HEADS = 4
SB_HD = BRANCH_W // SB_HEADS
DIFF_HEADS = 4
DIFF_KV_HEADS = 2
DIFF_HD = BRANCH_W // (2 * DIFF_HEADS)
MEM_HEADS = 4
MEM_HD = BRANCH_W // MEM_HEADS
MEM_LEN = 256
N_BUCKETS = 32
MAX_DISTANCE = 128
Q_BLOCK = 128
EPS = 1e-6
NEG = -1e30
FORGET_BIAS = 2.0

SPLIT_SIZES = (
    FOX_HEADS * FOX_HD, FOX_KV_HEADS * FOX_HD, FOX_KV_HEADS * FOX_HD, FOX_HEADS,
    SB_HEADS * SB_HD, SB_KV_HEADS * SB_HD, SB_KV_HEADS * SB_HD,
    DIFF_HEADS * 2 * DIFF_HD, DIFF_KV_HEADS * 2 * DIFF_HD, DIFF_KV_HEADS * 2 * DIFF_HD,
    MEM_HEADS * MEM_HD,
    N_BRANCH * BRANCH_W,
    N_BRANCH * D_MODEL,
)
SPLIT_POINTS = tuple(sum(SPLIT_SIZES[:i + 1]) for i in range(len(SPLIT_SIZES) - 1))
N_IN = sum(SPLIT_SIZES)

kernel_name = 'hybrid_fox_stickbreak_diff_memory_decoder_step'


def rmsnorm(x, g):
    xf = x.astype(jnp.float32)
    y = xf * lax.rsqrt(jnp.mean(xf * xf, axis=-1, keepdims=True) + EPS)
    return (y * g.astype(jnp.float32)).astype(x.dtype)


def t5_bucket(dist):
    max_exact = N_BUCKETS // 2
    large = max_exact + (jnp.log(jnp.maximum(dist, 1).astype(jnp.float32) / max_exact)
                         / math.log(MAX_DISTANCE / max_exact) * (N_BUCKETS - max_exact)).astype(jnp.int32)
    large = jnp.minimum(large, N_BUCKETS - 1)
    return jnp.where(dist < max_exact, dist, large)


def fox_attend(q, k, v, fq, fk, q_pos, k_pos):
    B, Tq, H, d = q.shape
    Tk, KVH = k.shape[1], k.shape[2]
    G = H // KVH
    s = jnp.einsum('bqkgd,bskd->bkgqs', q.reshape(B, Tq, KVH, G, d), k,
                   preferred_element_type=jnp.float32) * d ** -0.5
    decay = (fq.reshape(B, Tq, KVH, G).transpose(0, 2, 3, 1)[..., None]
             - fk.reshape(B, Tk, KVH, G).transpose(0, 2, 3, 1)[..., None, :])
    mask = k_pos[None, :] <= q_pos[:, None]
    p = jax.nn.softmax(jnp.where(mask, s + decay, NEG), axis=-1)
    o = jnp.einsum('bkgqs,bskd->bqkgd', p.astype(v.dtype), v)
    return o.reshape(B, Tq, H, v.shape[-1])


def sb_attend(q, k, v, q_pos, k_pos):
    B, Tq, H, d = q.shape
    KVH = k.shape[2]
    G = H // KVH
    z = jnp.einsum('bqkgd,bskd->bkgqs', q.reshape(B, Tq, KVH, G, d), k,
                   preferred_element_type=jnp.float32) * d ** -0.5
    mask = k_pos[None, :] < q_pos[:, None]
    log_1m = jnp.where(mask, jax.nn.log_sigmoid(-z), 0.0)
    rev = lax.cumsum(log_1m, axis=z.ndim - 1, reverse=True) - log_1m
    w = jnp.where(mask, jnp.exp(jax.nn.log_sigmoid(z) + rev), 0.0)
    o = jnp.einsum('bkgqs,bskd->bqkgd', w.astype(v.dtype), v)
    return o.reshape(B, Tq, H, v.shape[-1])


def diff_attend(q, k, v, q_pos, k_pos, rel_bias, lam):
    B, Tq, H, _, d = q.shape
    Tk, KVH = k.shape[1], k.shape[2]
    G = H // KVH
    s = jnp.einsum('bqkgmd,bskmd->bkgmqs', q.reshape(B, Tq, KVH, G, 2, d), k,
                   preferred_element_type=jnp.float32) * d ** -0.5
    dist = jnp.maximum(q_pos[:, None] - k_pos[None, :], 0)
    bias = rel_bias.astype(jnp.float32)[t5_bucket(dist)]
    bias = bias.transpose(2, 0, 1).reshape(KVH, G, 1, Tq, Tk)
    mask = k_pos[None, :] <= q_pos[:, None]
    p = jax.nn.softmax(jnp.where(mask, s + bias, NEG), axis=-1)
    w = p[:, :, :, 0] - lam * p[:, :, :, 1]
    o = jnp.einsum('bkgqs,bskd->bqkgd', w.astype(v.dtype), v)
    return o.reshape(B, Tq, H, v.shape[-1])


def diff_lambda(lam_l, lam_init):
    lf = lam_l.astype(jnp.float32)
    return jnp.exp(jnp.sum(lf[0] * lf[1])) - jnp.exp(jnp.sum(lf[2] * lf[3])) + lam_init


def diff_finish(o, g_sub, lam_init):
    return rmsnorm(o, g_sub) * (1.0 - lam_init)


def mem_attend(q, mem_kv):
    s = jnp.einsum('bqhd,bshd->bhqs', q, mem_kv[:, :, 0],
                   preferred_element_type=jnp.float32) * q.shape[-1] ** -0.5
    p = jax.nn.softmax(s, axis=-1)
    return jnp.einsum('bhqs,bshd->bqhd', p.astype(q.dtype), mem_kv[:, :, 1])


def sweep_blocks(fn, q_args, q_pos):
    nb = q_pos.shape[0] // Q_BLOCK

    def to_blocks(a):
        return jnp.moveaxis(a.reshape(a.shape[0], nb, Q_BLOCK, *a.shape[2:]), 1, 0)

    xs = tuple(to_blocks(a) for a in q_args) + (q_pos.reshape(nb, Q_BLOCK),)
    out = lax.map(lambda args: fn(*args), xs)
    out = jnp.moveaxis(out, 0, 1)
    return out.reshape(out.shape[0], nb * Q_BLOCK, *out.shape[3:])


def project_in(h, w_in, b_f):
    B, T, _ = h.shape
    (fq, fk, fv, ff, sq, sk, sv, dq, dk, dv, mq, gate_path, merge_gate) = jnp.split(h @ w_in, SPLIT_POINTS, axis=-1)
    logf = jax.nn.log_sigmoid((ff + b_f).astype(jnp.float32))
    fox = (fq.reshape(B, T, FOX_HEADS, FOX_HD), fk.reshape(B, T, FOX_KV_HEADS, FOX_HD),
           fv.reshape(B, T, FOX_KV_HEADS, FOX_HD), logf)
    sb = (sq.reshape(B, T, SB_HEADS, SB_HD), sk.reshape(B, T, SB_KV_HEADS, SB_HD),
          sv.reshape(B, T, SB_KV_HEADS, SB_HD))
    diff = (dq.reshape(B, T, DIFF_HEADS, 2, DIFF_HD), dk.reshape(B, T, DIFF_KV_HEADS, 2, DIFF_HD),
            dv.reshape(B, T, DIFF_KV_HEADS, 2 * DIFF_HD))
    return fox, sb, diff, mq.reshape(B, T, MEM_HEADS, MEM_HD), gate_path, merge_gate


def merge_branches(outs, gate_path, merge_gate, w_branch, w_out):
    B, T, _ = gate_path.shape
    o = jnp.stack([a.reshape(B, T, BRANCH_W) for a in outs], axis=2)
    o = o * jax.nn.silu(gate_path.reshape(B, T, N_BRANCH, BRANCH_W))
    u = jnp.einsum('btnw,nwd->btnd', o, w_branch)
    g = jax.nn.sigmoid(merge_gate.reshape(B, T, N_BRANCH, D_MODEL))
    return jnp.einsum('btd,de->bte', jnp.sum(g * u, axis=2), w_out)


def gather_pages(pool, page_table):
    g = pool[page_table]
    return g.reshape(g.shape[0], g.shape[1] * g.shape[2], *g.shape[3:])


def prompt_layer(x, mem, rel_bias, g_pre, g_post, g_mem, w_in, b_f, w_mem_kv, lam_l, lam_init,
                 g_sub, w_branch, w_out):
    B, S, _ = x.shape
    pos = jnp.arange(S, dtype=jnp.int32)
    h = rmsnorm(x, g_pre)
    (fq, fk, fv, logf), (sq, sk, sv), (dq, dk, dv), mq, gp, mg = project_in(h, w_in, b_f)
    F = jnp.cumsum(logf, axis=1)
    o_fox = sweep_blocks(lambda qb, fb, pb: fox_attend(qb, fk, fv, fb, F, pb, pos), (fq, F), pos)
    o_sb = sweep_blocks(lambda qb, pb: sb_attend(qb, sk, sv, pb, pos), (sq,), pos)
    o_diff = sweep_blocks(lambda qb, pb: diff_attend(qb, dk, dv, pb, pos, rel_bias, lam_l), (dq,), pos)
    mem_kv = (rmsnorm(mem, g_mem) @ w_mem_kv).reshape(B, mem.shape[1], 2, MEM_HEADS, MEM_HD)
    o_mem = mem_attend(mq, mem_kv)
    y = merge_branches((o_fox, o_sb, diff_finish(o_diff, g_sub, lam_init), o_mem), gp, mg, w_branch, w_out)
    x = x + rmsnorm(y, g_post)
    state = (jnp.stack([fk, fv], axis=2), logf, jnp.stack([sk, sv], axis=2),
             jnp.stack([dk.reshape(B, S, DIFF_KV_HEADS, 2 * DIFF_HD), dv], axis=2), mem_kv)
    return x, state


def sample_layer(x, page_table, fox_pool, logf_pool, sb_pool, diff_pool, mem_kv, rel_bias,
                 g_pre, g_post, w_in, b_f, lam_l, lam_init, g_sub, w_branch, w_out):
    B, T, _ = x.shape
    past_len = page_table.shape[1] * fox_pool.shape[1]
    q_pos = past_len + jnp.arange(T, dtype=jnp.int32)
    k_pos = jnp.arange(past_len + T, dtype=jnp.int32)
    h = rmsnorm(x, g_pre)
    (fq, fk, fv, logf), (sq, sk, sv), (dq, dk, dv), mq, gp, mg = project_in(h, w_in, b_f)
    fox_past = gather_pages(fox_pool, page_table)
    fk_all = jnp.concatenate([fox_past[:, :, 0], fk], axis=1)
    fv_all = jnp.concatenate([fox_past[:, :, 1], fv], axis=1)
    F_all = jnp.cumsum(jnp.concatenate([gather_pages(logf_pool, page_table).astype(jnp.float32), logf], axis=1), axis=1)
    o_fox = fox_attend(fq, fk_all, fv_all, F_all[:, past_len:], F_all, q_pos, k_pos)
    sb_past = gather_pages(sb_pool, page_table)
    o_sb = sb_attend(sq, jnp.concatenate([sb_past[:, :, 0], sk], axis=1),
                     jnp.concatenate([sb_past[:, :, 1], sv], axis=1), q_pos, k_pos)
    diff_past = gather_pages(diff_pool, page_table)
    dk_all = jnp.concatenate([diff_past[:, :, 0].reshape(B, past_len, DIFF_KV_HEADS, 2, DIFF_HD), dk], axis=1)
    dv_all = jnp.concatenate([diff_past[:, :, 1], dv], axis=1)
    o_diff = diff_attend(dq, dk_all, dv_all, q_pos, k_pos, rel_bias, lam_l)
    o_mem = mem_attend(mq, mem_kv)
    y = merge_branches((o_fox, o_sb, diff_finish(o_diff, g_sub, lam_init), o_mem), gp, mg, w_branch, w_out)
    x = x + rmsnorm(y, g_post)
    state = (jnp.stack([fk, fv], axis=2), logf, jnp.stack([sk, sv], axis=2),
             jnp.stack([dk.reshape(B, T, DIFF_KV_HEADS, 2 * DIFF_HD), dv], axis=2))
    return x, state


def setup_inputs(seed: int = 0) -> dict:
    key = jax.random.key(seed)
    ks = jax.random.split(key, 20)
    n_pages = PAST_LEN // PAGE_SIZE
    n_pool = (5 * DEC_BATCH * n_pages + 3) // 4
    f32 = jnp.float32

    def nrm(k, shape, scale=1.0):
        return scale * jax.random.normal(k, shape, f32)

    page_table = jax.random.permutation(ks[0], n_pool)[:DEC_BATCH * n_pages].reshape(DEC_BATCH, n_pages).astype(jnp.int32)
    return {
        'x_prompt': nrm(ks[1], (BATCH, SEQ, D_MODEL)),
        'x_sample': nrm(ks[2], (DEC_BATCH, DEC_SEQ, D_MODEL)),
        'cache_fox_kv': nrm(ks[3], (DEPTH, n_pool, PAGE_SIZE, 2, FOX_KV_HEADS, FOX_HD)),
        'cache_fox_logf': jax.nn.log_sigmoid(FORGET_BIAS + nrm(ks[4], (DEPTH, n_pool, PAGE_SIZE, FOX_HEADS))),
        'cache_sb_kv': nrm(ks[5], (DEPTH, n_pool, PAGE_SIZE, 2, SB_KV_HEADS, SB_HD)),
        'cache_diff_kv': nrm(ks[6], (DEPTH, n_pool, PAGE_SIZE, 2, DIFF_KV_HEADS, 2 * DIFF_HD)),
        'cache_mem_kv': nrm(ks[7], (DEPTH, DEC_BATCH, MEM_LEN, 2, MEM_HEADS, MEM_HD)),
        'page_table': page_table,
        'mem_prompt': nrm(ks[8], (BATCH, MEM_LEN, D_MODEL)),
        'rel_bias': nrm(ks[9], (N_BUCKETS, DIFF_HEADS), 0.3),
        'g_pre': 1.0 + nrm(ks[10], (DEPTH, D_MODEL), 0.02),
        'g_post': 1.0 + nrm(ks[11], (DEPTH, D_MODEL), 0.02),
        'g_mem': 1.0 + nrm(ks[12], (DEPTH, D_MODEL), 0.02),
        'w_in': nrm(ks[13], (DEPTH, D_MODEL, N_IN), D_MODEL ** -0.5),
        'b_f': FORGET_BIAS + nrm(ks[14], (DEPTH, FOX_HEADS), 0.1),
        'w_mem_kv': nrm(ks[15], (DEPTH, D_MODEL, 2 * MEM_HEADS * MEM_HD), D_MODEL ** -0.5),
        'lam': nrm(ks[16], (DEPTH, 4, DIFF_HD), 0.1),
        'g_diff_sub': 1.0 + nrm(ks[17], (DEPTH, 2 * DIFF_HD), 0.02),
        'w_branch': nrm(ks[18], (DEPTH, N_BRANCH, BRANCH_W, D_MODEL), BRANCH_W ** -0.5),
        'w_out': nrm(ks[19], (DEPTH, D_MODEL, D_MODEL), D_MODEL ** -0.5),
    }


def reference(x_prompt, x_sample, cache_fox_kv, cache_fox_logf, cache_sb_kv, cache_diff_kv, cache_mem_kv,
              page_table, mem_prompt, rel_bias, g_pre, g_post, g_mem, w_in, b_f, w_mem_kv, lam,
              g_diff_sub, w_branch, w_out):
    xp, xs = x_prompt, x_sample
    prompt_states, sample_states = [], []
    for l in range(DEPTH):
        lam_init = 0.8 - 0.6 * math.exp(-0.3 * l)
        lam_l = diff_lambda(lam[l], lam_init)
        xp, st_p = prompt_layer(xp, mem_prompt, rel_bias, g_pre[l], g_post[l], g_mem[l], w_in[l], b_f[l],
                                w_mem_kv[l], lam_l, lam_init, g_diff_sub[l], w_branch[l], w_out[l])
        xs, st_s = sample_layer(xs, page_table, cache_fox_kv[l], cache_fox_logf[l], cache_sb_kv[l],
                                cache_diff_kv[l], cache_mem_kv[l], rel_bias, g_pre[l], g_post[l], w_in[l],
                                b_f[l], lam_l, lam_init, g_diff_sub[l], w_branch[l], w_out[l])
        prompt_states.append(st_p)
        sample_states.append(st_s)
    p_fox_kv, p_fox_logf, p_sb_kv, p_diff_kv, p_mem_kv = [jnp.stack(s, axis=0) for s in zip(*prompt_states)]
    s_fox_kv, s_fox_logf, s_sb_kv, s_diff_kv = [jnp.stack(s, axis=0) for s in zip(*sample_states)]
    return (xp, xs, p_fox_kv, p_fox_logf, p_sb_kv, p_diff_kv, p_mem_kv, s_fox_kv, s_fox_logf, s_sb_kv, s_diff_kv)
```

```python
import functools
import math

import jax
import jax.numpy as jnp
from jax import lax
from jax.experimental import pallas as pl
from jax.experimental.pallas import tpu as pltpu

F32 = jnp.float32
BF16 = jnp.bfloat16

D_MODEL = 1024
N_BRANCH = 4
BRANCH_W = D_MODEL // 2
FOX_HEADS = 8
HD = 64
DIFF_HEADS = 4
MEM_HEADS = 4
MEM_HD = 128
N_BUCKETS = 32
MAX_DISTANCE = 128
EPS = 1e-6
NEG = -1e30
PAGE = 128
LANES = 128
VMEM_LIMIT = 56 * 1024 * 1024

_C_FQ, _C_FKV, _C_SQ, _C_SKV, _C_DQ, _C_DKV, _C_MQ, _C_FF = (
    0, 512, 1024, 1536, 2048, 2560, 3072, 3584)
_N_PROJ = 3712


def _nt(a, b):
    return lax.dot_general(a, b, (((1,), (1,)), ((), ())), preferred_element_type=F32)


def _mm(a, b):
    return jnp.dot(a, b, preferred_element_type=F32)


def _split_bf16(x, parts):
    out = []
    r = x
    for i in range(parts):
        p = r.astype(BF16)
        out.append(p)
        if i + 1 < parts:
            r = r - p.astype(F32)
    return out


def _log_sigmoid(x):
    return jnp.minimum(x, 0.0) - jnp.log1p(jnp.exp(-jnp.abs(x)))


def _sigmoid(x):
    return 1.0 / (1.0 + jnp.exp(-x))


def _rms(x, g):
    return x * lax.rsqrt(jnp.mean(x * x, axis=-1, keepdims=True) + EPS) * g


def _cparams(sem, vmem=None):
    return pltpu.CompilerParams(dimension_semantics=sem, vmem_limit_bytes=vmem)


def _proj_kernel(x_ref, g_ref, w_ref, wft_ref, bf_ref, bft_ref,
                 qf_ref, kvfb_ref, kvf_ref, lf_ref, lft_ref,
                 qs_ref, kvsb_ref, kvs_ref, qd_ref, kvdb_ref, kvd_ref, qm_ref):
    hb = _rms(x_ref[...], g_ref[...]).astype(BF16)
    lane_hi = lax.broadcasted_iota(jnp.int32, (hb.shape[0], LANES), 1) >= HD

    def proj(c0, n):
        return _mm(hb, w_ref[:, c0:c0 + n])

    def pad_heads(q, out_ref):
        for h in range(FOX_HEADS):
            grp = q[:, LANES * (h // 2):LANES * (h // 2 + 1)]
            dst_hi = (h // 2) % 2 == 1
            if (h % 2 == 1) != dst_hi:
                grp = pltpu.roll(grp, HD, 1)
            keep = lane_hi if dst_hi else jnp.logical_not(lane_hi)
            out_ref[:, LANES * h:LANES * (h + 1)] = jnp.where(keep, grp, 0.0).astype(BF16)

    pad_heads(proj(_C_FQ, 512) * 0.125, qf_ref)
    kv = proj(_C_FKV, 512)
    kvf_ref[...] = kv
    kvfb_ref[...] = kv.astype(BF16)
    pad_heads(proj(_C_SQ, 512) * 0.125, qs_ref)
    kv = proj(_C_SKV, 512)
    kvs_ref[...] = kv
    kvsb_ref[...] = kv.astype(BF16)
    qd_ref[...] = (proj(_C_DQ, 512) * 0.125).astype(BF16)
    kv = proj(_C_DKV, 512)
    kvd_ref[...] = kv
    kvdb_ref[...] = kv.astype(BF16)
    qm_ref[...] = proj(_C_MQ, 512).astype(BF16)
    ff = proj(_C_FF, LANES)[:, 0:FOX_HEADS] + bf_ref[...]
    lf_ref[...] = _log_sigmoid(ff)
    lft_ref[...] = _log_sigmoid(_nt(wft_ref[...], hb) + bft_ref[...])


def _proj_call(x2, g, w, wft, bf, bft, tm):
    T = x2.shape[0]
    row = lambda n: pl.BlockSpec((tm, n), lambda i: (i, 0))
    full = lambda a: pl.BlockSpec(a.shape, lambda i: (0,) * a.ndim)
    sds = jax.ShapeDtypeStruct
    out_shape = (
        sds((T, 1024), BF16), sds((T, 512), BF16), sds((T, 512), F32),
        sds((T, FOX_HEADS), F32), sds((FOX_HEADS, T), F32),
        sds((T, 1024), BF16), sds((T, 512), BF16), sds((T, 512), F32),
        sds((T, 512), BF16), sds((T, 512), BF16), sds((T, 512), F32),
        sds((T, 512), BF16))
    out_specs = (
        row(1024), row(512), row(512), row(FOX_HEADS),
        pl.BlockSpec((FOX_HEADS, tm), lambda i: (0, i)),
        row(1024), row(512), row(512), row(512), row(512), row(512), row(512))
    return pl.pallas_call(
        _proj_kernel, grid=(T // tm,),
        in_specs=[row(D_MODEL), full(g), full(w), full(wft), full(bf), full(bft)],
        out_specs=out_specs, out_shape=out_shape,
        compiler_params=_cparams(("parallel",), VMEM_LIMIT),
    )(x2, g, w, wft, bf, bft)


def _memproj_kernel(x_ref, g_ref, w_ref, kv_ref, kvb_ref):
    hb = _rms(x_ref[...], g_ref[...]).astype(BF16)
    kv = _mm(hb, w_ref[...])
    kv_ref[...] = kv
    kvb_ref[...] = kv.astype(BF16)


def _memproj_call(x2, g, w, tm):
    T, N = x2.shape[0], w.shape[1]
    return pl.pallas_call(
        _memproj_kernel, grid=(T // tm,),
        in_specs=[pl.BlockSpec((tm, D_MODEL), lambda i: (i, 0)),
                  pl.BlockSpec(g.shape, lambda i: (0, 0)),
                  pl.BlockSpec(w.shape, lambda i: (0, 0))],
        out_specs=(pl.BlockSpec((tm, N), lambda i: (i, 0)),) * 2,
        out_shape=(jax.ShapeDtypeStruct((T, N), F32), jax.ShapeDtypeStruct((T, N), BF16)),
        compiler_params=_cparams(("parallel",), VMEM_LIMIT),
    )(x2, g, w)


def _cumsum_kernel(lft_ref, f_ref, *, chunk):
    S = lft_ref.shape[1]
    j = lax.broadcasted_iota(jnp.int32, (chunk, chunk), 0)
    s = lax.broadcasted_iota(jnp.int32, (chunk, chunk), 1)
    tri = (j <= s).astype(BF16)
    carry = jnp.zeros((FOX_HEADS, 1), F32)
    for c in range(S // chunk):
        x = lft_ref[:, c * chunk:(c + 1) * chunk]
        loc = sum(_mm(p, tri) for p in _split_bf16(x, 3))
        f_ref[:, c * chunk:(c + 1) * chunk] = loc + carry
        carry = carry + loc[:, chunk - 1:chunk]


def _cumsum_call(lft, B, S):
    chunk = min(256, S)
    return pl.pallas_call(
        functools.partial(_cumsum_kernel, chunk=chunk), grid=(B,),
        in_specs=[pl.BlockSpec((FOX_HEADS, S), lambda b: (0, b))],
        out_specs=pl.BlockSpec((None, FOX_HEADS, S), lambda b: (b, 0, 0)),
        out_shape=jax.ShapeDtypeStruct((B, FOX_HEADS, S), F32),
        compiler_params=_cparams(("parallel",)),
    )(lft)


def _t5_bucket(dist):
    max_exact = N_BUCKETS // 2
    d1 = jnp.maximum(dist, 1).astype(F32)
    large = max_exact + (jnp.log(d1 / max_exact) / math.log(MAX_DISTANCE / max_exact)
                         * (N_BUCKETS - max_exact)).astype(jnp.int32)
    large = jnp.minimum(large, N_BUCKETS - 1)
    return jnp.where(dist < max_exact, dist, large)


def _bias_kernel(rb_ref, tiles_ref, dec_ref, *, tile):
    i = lax.broadcasted_iota(jnp.int32, (tile, tile), 0)
    j = lax.broadcasted_iota(jnp.int32, (tile, tile), 1)
    for d in range(3):
        bkt = _t5_bucket(jnp.maximum(tile * d + i - j, 0))
        for h in range(DIFF_HEADS):
            acc = jnp.zeros((tile, tile), F32)
            for b in range(N_BUCKETS):
                acc = jnp.where(bkt == b, rb_ref[b, h], acc)
            tiles_ref[d, h] = acc
    r = lax.broadcasted_iota(jnp.int32, (2 * DIFF_HEADS, LANES), 0) % DIFF_HEADS
    lane = lax.broadcasted_iota(jnp.int32, (2 * DIFF_HEADS, LANES), 1)
    for idx, dist in enumerate((PAGE - lane, jnp.full_like(lane, 2 * MAX_DISTANCE),
                                jnp.zeros_like(lane))):
        bkt = _t5_bucket(dist)
        acc = jnp.zeros((2 * DIFF_HEADS, LANES), F32)
        for h in range(DIFF_HEADS):
            for b in range(N_BUCKETS):
                acc = jnp.where((bkt == b) & (r == h), rb_ref[b, h], acc)
        dec_ref[idx] = acc


def _bias_call(rel_bias, tile):
    return pl.pallas_call(
        functools.partial(_bias_kernel, tile=tile),
        in_specs=[pl.BlockSpec(memory_space=pltpu.SMEM)],
        out_specs=(pl.BlockSpec(memory_space=pltpu.VMEM),) * 2,
        out_shape=(jax.ShapeDtypeStruct((3, DIFF_HEADS, tile, tile), F32),
                   jax.ShapeDtypeStruct((3, 2 * DIFF_HEADS, LANES), F32)),
    )(rel_bias)


def _causal_mask(t):
    return (lax.broadcasted_iota(jnp.int32, (t, t), 1)
            <= lax.broadcasted_iota(jnp.int32, (t, t), 0))


def _softmax_step(s, v, carry):
    m, l, acc = carry
    m_new = jnp.maximum(m, jnp.max(s, axis=-1, keepdims=True))
    alpha = jnp.exp(m - m_new)
    p = jnp.exp(s - m_new)
    l = alpha * l + jnp.sum(p, axis=-1, keepdims=True)
    acc = alpha * acc + _mm(p.astype(BF16), v)
    return m_new, l, acc


def _softmax_init(t):
    return (jnp.full((t, 1), NEG, F32), jnp.zeros((t, 1), F32), jnp.zeros((t, LANES), F32))


def _fox_kernel(q_ref, k_ref, v_ref, fcol_ref, frow_ref, o_ref, *, t):
    qi = pl.program_id(1)
    mask = _causal_mask(t)
    for h in range(FOX_HEADS):
        kvh = h // 2
        lo = LANES * (kvh // 2)
        q = q_ref[:, LANES * h:LANES * (h + 1)]
        fq = fcol_ref[:, h:h + 1]

        def step(kb, carry, masked, q=q, fq=fq, lo=lo, h=h):
            ks = pl.multiple_of(kb * t, t)
            s = _nt(q, k_ref[pl.ds(ks, t), lo:lo + LANES])
            s = s + (fq - frow_ref[h:h + 1, pl.ds(ks, t)])
            if masked:
                s = jnp.where(mask, s, NEG)
            return _softmax_step(s, v_ref[pl.ds(ks, t), lo:lo + LANES], carry)

        carry = lax.fori_loop(0, qi, functools.partial(step, masked=False), _softmax_init(t))
        _, l, acc = step(qi, carry, True)
        o = acc / l
        half = HD * (kvh % 2)
        o_ref[:, HD * h:HD * (h + 1)] = o[:, half:half + HD]


def _sb_kernel(q_ref, k_ref, v_ref, o_ref, *, t):
    qi = pl.program_id(1)
    strict = (lax.broadcasted_iota(jnp.int32, (t, t), 1)
              < lax.broadcasted_iota(jnp.int32, (t, t), 0))
    tri = (lax.broadcasted_iota(jnp.int32, (t, t), 0)
           > lax.broadcasted_iota(jnp.int32, (t, t), 1)).astype(BF16)
    for h in range(FOX_HEADS):
        kvh = h // 2
        lo = LANES * (kvh // 2)
        q = q_ref[:, LANES * h:LANES * (h + 1)]

        def step(kb, carry, masked, q=q, lo=lo):
            run, acc = carry
            ks = pl.multiple_of(kb * t, t)
            z = _nt(q, k_ref[pl.ds(ks, t), lo:lo + LANES])
            sp = jnp.maximum(z, 0.0) + jnp.log(1.0 + jnp.exp(-jnp.abs(z)))
            l1m = -sp
            if masked:
                l1m = jnp.where(strict, l1m, 0.0)
            rev = sum(_mm(p, tri) for p in _split_bf16(l1m, 2))
            w = jnp.exp((z - sp) + rev + run)
            if masked:
                w = jnp.where(strict, w, 0.0)
            acc = acc + _mm(w.astype(BF16), v_ref[pl.ds(ks, t), lo:lo + LANES])
            run = run + rev[:, 0:1] + l1m[:, 0:1]
            return run, acc

        carry = step(qi, (jnp.zeros((t, 1), F32), jnp.zeros((t, LANES), F32)), True)
        _, acc = lax.fori_loop(
            0, qi, lambda i, c: step(qi - 1 - i, c, False), carry)
        half = HD * (kvh % 2)
        o_ref[:, HD * h:HD * (h + 1)] = acc[:, half:half + HD]


def _diff_lambda(lam_ref, lam_init):
    lf = lam_ref[...]
    a = jnp.sum(lf[0:1] * lf[1:2], axis=-1, keepdims=True)
    b = jnp.sum(lf[2:3] * lf[3:4], axis=-1, keepdims=True)
    return jnp.exp(a) - jnp.exp(b) + lam_init


def _diff_kernel(q_ref, k_ref, v_ref, bias_ref, lam_ref, gsub_ref, o_ref, *, t, lam_init):
    qi = pl.program_id(1)
    mask = _causal_mask(t)
    lane_hi = lax.broadcasted_iota(jnp.int32, (t, LANES), 1) >= HD
    lam = _diff_lambda(lam_ref, lam_init)
    for h in range(DIFF_HEADS):
        lo = LANES * (h // 2)
        q = q_ref[:, LANES * h:LANES * (h + 1)]
        zero = jnp.zeros_like(q)
        q0 = jnp.where(lane_hi, zero, q)
        q1 = jnp.where(lane_hi, q, zero)

        def step(kb, carry, masked, q0=q0, q1=q1, lo=lo, h=h):
            ks = pl.multiple_of(kb * t, t)
            k = k_ref[pl.ds(ks, t), lo:lo + LANES]
            v = v_ref[pl.ds(ks, t), lo:lo + LANES]
            bias = bias_ref[jnp.minimum(qi - kb, 2), h]
            out = []
            for qm, c in zip((q0, q1), carry):
                s = _nt(qm, k) + bias
                if masked:
                    s = jnp.where(mask, s, NEG)
                out.append(_softmax_step(s, v, c))
            return tuple(out)

        init = (_softmax_init(t), _softmax_init(t))
        carry = lax.fori_loop(0, qi, functools.partial(step, masked=False), init)
        (_, l0, a0), (_, l1, a1) = step(qi, carry, True)
        o = a0 / l0 - lam * (a1 / l1)
        o_ref[:, LANES * h:LANES * (h + 1)] = _rms(o, gsub_ref[...]) * (1.0 - lam_init)


def _mem_kernel(q_ref, k_ref, v_ref, o_ref):
    for h in range(MEM_HEADS):
        sl = slice(MEM_HD * h, MEM_HD * (h + 1))
        s = _nt(q_ref[:, sl], k_ref[:, sl]) * MEM_HD ** -0.5
        e = jnp.exp(s - jnp.max(s, axis=-1, keepdims=True))
        p = e / jnp.sum(e, axis=-1, keepdims=True)
        o_ref[:, sl] = _mm(p.astype(BF16), v_ref[:, sl])


def _prompt_attn_call(body, q, kv, extra, extra_specs, B, S, t, qw):
    half = kv.shape[2] // 2
    return pl.pallas_call(
        body, grid=(B, S // t),
        in_specs=[pl.BlockSpec((None, t, qw), lambda b, i: (b, i, 0)),
                  pl.BlockSpec((None, kv.shape[1], half), lambda b, i: (b, 0, 0)),
                  pl.BlockSpec((None, kv.shape[1], half), lambda b, i: (b, 0, 1))] + extra_specs,
        out_specs=pl.BlockSpec((None, t, BRANCH_W), lambda b, i: (b, i, 0)),
        out_shape=jax.ShapeDtypeStruct((B, S, BRANCH_W), F32),
        compiler_params=_cparams(("parallel", "arbitrary"), VMEM_LIMIT),
    )(q, kv, kv, *extra)


def _merge_kernel(x_ref, gpre_ref, gpost_ref, of_ref, os_ref, od_ref, om_ref,
                  wgp_ref, wmg_ref, wbr_ref, wout_ref, y_ref):
    x = x_ref[...]
    hb = _rms(x, gpre_ref[...]).astype(BF16)
    y = jnp.zeros(x.shape, F32)
    for n, o_ref in enumerate((of_ref, os_ref, od_ref, om_ref)):
        gp = _mm(hb, wgp_ref[:, BRANCH_W * n:BRANCH_W * (n + 1)])
        o = o_ref[...] * (gp * _sigmoid(gp))
        u = _mm(o.astype(BF16), wbr_ref[n])
        mg = _mm(hb, wmg_ref[:, D_MODEL * n:D_MODEL * (n + 1)])
        y = y + _sigmoid(mg) * u
    out = _mm(y.astype(BF16), wout_ref[...])
    y_ref[...] = x + _rms(out, gpost_ref[...])


def _merge_call(x2, gpre, gpost, outs, wgp, wmg, wbr, wout, tm):
    T = x2.shape[0]
    row = lambda n: pl.BlockSpec((tm, n), lambda i: (i, 0))

    def const(a):
        return pl.BlockSpec(a.shape, lambda i: (0,) * a.ndim, pipeline_mode=pl.Buffered(1))

    return pl.pallas_call(
        _merge_kernel, grid=(T // tm,),
        in_specs=[row(D_MODEL), const(gpre), const(gpost)] + [row(BRANCH_W)] * 4
        + [const(wgp), const(wmg), const(wbr), const(wout)],
        out_specs=row(D_MODEL),
        out_shape=jax.ShapeDtypeStruct((T, D_MODEL), F32),
        compiler_params=_cparams(("parallel",), VMEM_LIMIT),
    )(x2, gpre, gpost, *outs, wgp, wmg, wbr, wout)


def _suffix_scan(x):
    lane = lax.broadcasted_iota(jnp.int32, x.shape, 1)
    y = x
    d = 1
    while d < LANES:
        y = y + jnp.where(lane + d < LANES, pltpu.roll(y, LANES - d, 1), 0.0)
        d *= 2
    return y


def _decode_kernel(pt_ref, *refs, kind, pps, lam_init):
    del pt_ref
    it = iter(refs)
    q_ref, kvnew_ref = next(it), next(it)
    pages = [next(it) for _ in range(pps)]
    if kind == "fox":
        lfnew_ref = next(it)
        lfs = [next(it) for _ in range(pps)]
    if kind == "diff":
        dbias_ref, lam_ref, gsub_ref = next(it), next(it), next(it)
    o_ref = next(it)
    m_ref, l_ref, acc_ref, run_ref = next(it), next(it), next(it), next(it)
    g = pl.program_id(1)
    q = q_ref[...]
    qf = q.astype(F32)

    @pl.when(g == 0)
    def _():
        if kind == "sb":
            run_ref[...] = jnp.zeros_like(run_ref)
            acc_ref[...] = jnp.zeros_like(acc_ref)
        else:
            knew = kvnew_ref[:, 0:256].astype(BF16).astype(F32)
            vnew = kvnew_ref[:, 256:512].astype(BF16).astype(F32)
            s = jnp.sum(qf * knew, axis=-1, keepdims=True)
            if kind == "diff":
                s = s + dbias_ref[2][:, 0:1]
            m_ref[...] = s
            l_ref[...] = jnp.ones_like(l_ref)
            acc_ref[...] = jnp.broadcast_to(vnew, acc_ref.shape)
            if kind == "fox":
                run_ref[...] = lfnew_ref[...]

    for i in range(pps):
        page = pages[i]
        k = page[:, 0:256].astype(BF16)
        v = page[:, 256:512].astype(BF16)
        s = _nt(q, k)
        if kind == "sb":
            sp = jnp.maximum(s, 0.0) + jnp.log(1.0 + jnp.exp(-jnp.abs(s)))
            l1m = -sp
            inc = _suffix_scan(l1m)
            run = run_ref[...]
            w = jnp.exp((s - sp) + (inc - l1m) + run)
            acc_ref[...] = acc_ref[...] + _mm(w.astype(BF16), v)
            run_ref[...] = run + inc[:, 0:1]
            continue
        if kind == "fox":
            lf = lfs[i][...]
            inc = _suffix_scan(lf)
            run = run_ref[...]
            s = s + ((inc - lf) + run)
            run_ref[...] = run + inc[:, 0:1]
        else:
            if i == 0:
                s = s + jnp.where(g == 0, dbias_ref[0], dbias_ref[1])
            else:
                s = s + dbias_ref[1]
        m = m_ref[...]
        m_new = jnp.maximum(m, jnp.max(s, axis=-1, keepdims=True))
        alpha = jnp.exp(m - m_new)
        p = jnp.exp(s - m_new)
        l_ref[...] = alpha * l_ref[...] + jnp.sum(p, axis=-1, keepdims=True)
        acc_ref[...] = alpha * acc_ref[...] + _mm(p.astype(BF16), v)
        m_ref[...] = m_new

    @pl.when(g == pl.num_programs(1) - 1)
    def _():
        if kind == "diff":
            n = acc_ref[...] / l_ref[...]
            lam = _diff_lambda(lam_ref, lam_init)
            o = n[0:DIFF_HEADS] - lam * n[DIFF_HEADS:2 * DIFF_HEADS]
            row = lax.broadcasted_iota(jnp.int32, (DIFF_HEADS, LANES), 0)
            o = jnp.where(row // 2 == 0, o[:, 0:LANES], o[:, LANES:2 * LANES])
            o_ref[...] = _rms(o, gsub_ref[...]) * (1.0 - lam_init)
        else:
            o = acc_ref[...] if kind == "sb" else acc_ref[...] / l_ref[...]
            row = lax.broadcasted_iota(jnp.int32, (FOX_HEADS, HD), 0)
            res = o[:, 0:HD]
            for kvh in range(1, 4):
                res = jnp.where(row // 2 == kvh, o[:, HD * kvh:HD * (kvh + 1)], res)
            o_ref[...] = res


def _decode_call(kind, layer, pt, qbd, kvnew, pool, extra, lam_init, pps):
    B, n_pages = pt.shape
    steps = n_pages // pps
    oshape = (DIFF_HEADS, LANES) if kind == "diff" else (FOX_HEADS, HD)

    def page_map(i):
        return lambda b, g, pt: (layer, pt[b, n_pages - 1 - (g * pps + i)], 0, 0)

    per_b = lambda a: pl.BlockSpec((None,) + a.shape[1:], lambda b, g, pt: (b,) + (0,) * (a.ndim - 1))
    const = lambda a: pl.BlockSpec(a.shape, lambda b, g, pt: (0,) * a.ndim)
    in_specs = [per_b(qbd), per_b(kvnew)]
    in_specs += [pl.BlockSpec((None, None, PAGE, 512), page_map(i)) for i in range(pps)]
    args = [qbd, kvnew] + [pool] * pps
    if kind == "fox":
        lfnew, lfpool = extra
        in_specs += [per_b(lfnew)]
        in_specs += [pl.BlockSpec((None, None, FOX_HEADS, PAGE), page_map(i)) for i in range(pps)]
        args += [lfnew] + [lfpool] * pps
    elif kind == "diff":
        in_specs += [const(a) for a in extra]
        args += list(extra)
    grid_spec = pltpu.PrefetchScalarGridSpec(
        num_scalar_prefetch=1, grid=(B, steps), in_specs=in_specs,
        out_specs=pl.BlockSpec((None,) + oshape, lambda b, g, pt: (b, 0, 0)),
        scratch_shapes=[pltpu.VMEM((8, 1), F32), pltpu.VMEM((8, 1), F32),
                        pltpu.VMEM((8, 256), F32), pltpu.VMEM((8, 1), F32)])
    return pl.pallas_call(
        functools.partial(_decode_kernel, kind=kind, pps=pps, lam_init=lam_init),
        grid_spec=grid_spec,
        out_shape=jax.ShapeDtypeStruct((B,) + oshape, F32),
        compiler_params=_cparams(("parallel", "arbitrary"), VMEM_LIMIT),
    )(pt, *args)


def _memdec_kernel(q_ref, kv_ref, o_ref):
    k = kv_ref[:, 0:512].astype(BF16)
    v = kv_ref[:, 512:1024].astype(BF16)
    s = _nt(q_ref[...], k) * MEM_HD ** -0.5
    e = jnp.exp(s - jnp.max(s, axis=-1, keepdims=True))
    p = e / jnp.sum(e, axis=-1, keepdims=True)
    o = _mm(p.astype(BF16), v)
    row = lax.broadcasted_iota(jnp.int32, (8, MEM_HD), 0)
    res = o[:, 0:MEM_HD]
    for h in range(1, MEM_HEADS):
        res = jnp.where(row == h, o[:, MEM_HD * h:MEM_HD * (h + 1)], res)
    o_ref[...] = res[0:MEM_HEADS]


def _memdec_call(layer, qbd, cache):
    B = qbd.shape[0]
    M = cache.shape[2]
    return pl.pallas_call(
        _memdec_kernel, grid=(B,),
        in_specs=[pl.BlockSpec((None, 8, 512), lambda b: (b, 0, 0)),
                  pl.BlockSpec((None, None, M, 1024), lambda b: (layer, b, 0, 0))],
        out_specs=pl.BlockSpec((None, MEM_HEADS, MEM_HD), lambda b: (b, 0, 0)),
        out_shape=jax.ShapeDtypeStruct((B, MEM_HEADS, MEM_HD), F32),
        compiler_params=_cparams(("parallel",), VMEM_LIMIT),
    )(qbd, cache)


def _pick(n, pref):
    t = min(n, pref)
    while n % t:
        t //= 2
    return t


def _layer_weights(w_in_l, b_f_l, w_branch_l, w_out_l):
    c = (0, 512, 768, 1024, 1032, 1544, 1800, 2056, 2568, 2824, 3080, 3592, 5640, 9736)
    ff = jnp.pad(w_in_l[:, c[3]:c[4]], ((0, 0), (0, LANES - FOX_HEADS)))
    w_a = jnp.concatenate([w_in_l[:, c[0]:c[3]], w_in_l[:, c[4]:c[7]], w_in_l[:, c[7]:c[10]],
                           w_in_l[:, c[10]:c[11]], ff], axis=1).astype(BF16)
    wft = jnp.transpose(w_in_l[:, c[3]:c[4]]).astype(BF16)
    return dict(
        w_a=w_a, wft=wft, bf=b_f_l.reshape(1, FOX_HEADS), bft=b_f_l.reshape(FOX_HEADS, 1),
        wgp=w_in_l[:, c[11]:c[12]].astype(BF16), wmg=w_in_l[:, c[12]:c[13]].astype(BF16),
        wbr=w_branch_l.astype(BF16), wout=w_out_l.astype(BF16))


def kernel(x_prompt, x_sample, cache_fox_kv, cache_fox_logf, cache_sb_kv, cache_diff_kv, cache_mem_kv,
           page_table, mem_prompt, rel_bias, g_pre, g_post, g_mem, w_in, b_f, w_mem_kv, lam,
           g_diff_sub, w_branch, w_out):
    B, S, D = x_prompt.shape
    DB = x_sample.shape[0]
    depth = w_in.shape[0]
    n_pool = cache_fox_kv.shape[1]
    n_pages = page_table.shape[1]
    mem_len = mem_prompt.shape[1]
    T = B * S
    t_att = _pick(S, 256)
    tm_proj = _pick(T, 512)
    tm_merge = _pick(T, 256)
    pps = _pick(n_pages, 8)

    fox_pool = cache_fox_kv.reshape(depth, n_pool, PAGE, 512)
    sb_pool = cache_sb_kv.reshape(depth, n_pool, PAGE, 512)
    diff_pool = cache_diff_kv.reshape(depth, n_pool, PAGE, 512)
    lf_pool = jnp.swapaxes(cache_fox_logf, 2, 3)
    mem_cache = cache_mem_kv.reshape(depth, DB, mem_len, 1024)
    bias_tiles, dec_bias = _bias_call(rel_bias, t_att)

    h8 = jnp.arange(8)
    sel_pair = (jnp.arange(2)[None, :] == (h8 // 4)[:, None]).astype(BF16)
    r8m, r8h = h8 // 4, h8 % 4
    sel_diff = ((jnp.arange(2)[None, :, None] == (r8h // 2)[:, None, None])
                & (jnp.arange(2)[None, None, :] == r8m[:, None, None])).astype(BF16)
    sel_mem = (jnp.arange(8)[:, None] == jnp.arange(4)[None, :]).astype(BF16)

    xp = x_prompt.reshape(T, D)
    xs = x_sample.reshape(DB, D)
    mem2 = mem_prompt.reshape(B * mem_len, D)
    st_p = [[] for _ in range(5)]
    st_s = [[] for _ in range(4)]
    for l in range(depth):
        lam_init = 0.8 - 0.6 * math.exp(-0.3 * l)
        w = _layer_weights(w_in[l], b_f[l], w_branch[l], w_out[l])
        gpre, gpost = g_pre[l].reshape(1, D), g_post[l].reshape(1, D)
        gsub = g_diff_sub[l].reshape(1, 2 * HD)

        (qf, kvfb, kvf, lf, lft, qs, kvsb, kvs, qd, kvdb, kvd, qm) = _proj_call(
            xp, gpre, w["w_a"], w["wft"], w["bf"], w["bft"], tm_proj)
        mkv, mkvb = _memproj_call(mem2, g_mem[l].reshape(1, D), w_mem_kv[l].astype(BF16),
                                  _pick(B * mem_len, 512))
        f_row = _cumsum_call(lft, B, S)
        f_col = jnp.swapaxes(f_row, 1, 2)
        r3 = lambda a: a.reshape(B, S, a.shape[-1])
        o_fox = _prompt_attn_call(
            functools.partial(_fox_kernel, t=t_att), r3(qf), r3(kvfb), [f_col, f_row],
            [pl.BlockSpec((None, t_att, FOX_HEADS), lambda b, i: (b, i, 0)),
             pl.BlockSpec((None, FOX_HEADS, S), lambda b, i: (b, 0, 0))], B, S, t_att, 1024)
        o_sb = _prompt_attn_call(
            functools.partial(_sb_kernel, t=t_att), r3(qs), r3(kvsb), [], [], B, S, t_att, 1024)
        c4 = lambda a: pl.BlockSpec(a.shape, lambda b, i: (0,) * a.ndim)
        o_diff = _prompt_attn_call(
            functools.partial(_diff_kernel, t=t_att, lam_init=lam_init), r3(qd), r3(kvdb),
            [bias_tiles, lam[l], gsub], [c4(bias_tiles), c4(lam[l]), c4(gsub)], B, S, t_att, 512)
        o_mem = _prompt_attn_call(
            _mem_kernel, r3(qm), mkvb.reshape(B, mem_len, 1024), [], [], B, S, t_att, 512)
        flat = lambda a: a.reshape(T, BRANCH_W)
        xp = _merge_call(xp, gpre, gpost, [flat(o_fox), flat(o_sb), flat(o_diff), flat(o_mem)],
                         w["wgp"], w["wmg"], w["wbr"], w["wout"], tm_merge)
        st_p[0].append(kvf.reshape(B, S, 2, 4, HD))
        st_p[1].append(lf.reshape(B, S, FOX_HEADS))
        st_p[2].append(kvs.reshape(B, S, 2, 4, HD))
        st_p[3].append(kvd.reshape(B, S, 2, 2, 2 * HD))
        st_p[4].append(mkv.reshape(B, mem_len, 2, MEM_HEADS, MEM_HD))

        (qf, _, kvf, lf, lft, qs, _, kvs, qd, _, kvd, qm) = _proj_call(
            xs, gpre, w["w_a"], w["wft"], w["bf"], w["bft"], DB)
        place = lambda q: (q.reshape(DB, 8, 1, LANES) * sel_pair[None, :, :, None]).reshape(DB, 8, 256)
        qd_rows = jnp.tile(qd.reshape(DB, 1, DIFF_HEADS, 1, 2, HD), (1, 2, 1, 1, 1, 1)).reshape(DB, 8, 1, 2, HD)
        qbd_diff = (qd_rows * sel_diff[None, :, :, :, None]).reshape(DB, 8, 256)
        qbd_mem = (qm.reshape(DB, 1, MEM_HEADS, MEM_HD)
                   * sel_mem[None, :, :, None]).reshape(DB, 8, 512)
        new = lambda kv: kv.reshape(DB, 1, 512)
        a_fox = _decode_call("fox", l, page_table, place(qf), new(kvf), fox_pool,
                             (jnp.swapaxes(lft, 0, 1).reshape(DB, FOX_HEADS, 1), lf_pool), lam_init, pps)
        a_sb = _decode_call("sb", l, page_table, place(qs), new(kvs), sb_pool, (), lam_init, pps)
        a_diff = _decode_call("diff", l, page_table, qbd_diff, new(kvd), diff_pool,
                              (dec_bias, lam[l], gsub), lam_init, pps)
        a_mem = _memdec_call(l, qbd_mem, mem_cache)
        flat_s = lambda a: a.reshape(DB, BRANCH_W)
        xs = _merge_call(xs, gpre, gpost, [flat_s(a_fox), flat_s(a_sb), flat_s(a_diff), flat_s(a_mem)],
                         w["wgp"], w["wmg"], w["wbr"], w["wout"], DB)
        st_s[0].append(kvf.reshape(DB, 1, 2, 4, HD))
        st_s[1].append(lf.reshape(DB, 1, FOX_HEADS))
        st_s[2].append(kvs.reshape(DB, 1, 2, 4, HD))
        st_s[3].append(kvd.reshape(DB, 1, 2, 2, 2 * HD))

    return (xp.reshape(B, S, D), xs.reshape(DB, 1, D),
            *[jnp.stack(s, 0) for s in st_p], *[jnp.stack(s, 0) for s in st_s])
```

```python
import functools
import math

import jax
import jax.numpy as jnp
from jax import lax
from jax.experimental import pallas as pl
from jax.experimental.pallas import tpu as pltpu

F32 = jnp.float32
BF16 = jnp.bfloat16

D_MODEL = 1024
N_BRANCH = 4
BRANCH_W = D_MODEL // 2
FOX_HEADS = 8
HD = 64
DIFF_HEADS = 4
MEM_HEADS = 4
MEM_HD = 128
N_BUCKETS = 32
MAX_DISTANCE = 128
EPS = 1e-6
NEG = -1e30
PAGE = 128
LANES = 128
VMEM_LIMIT = 56 * 1024 * 1024
SB_DEAD = -104.0

_R_FQ, _R_FKV, _R_SQ, _R_SKV, _R_DQ, _R_DKV, _R_MQ = 0, 512, 1024, 1536, 2048, 2560, 3072


def _nt(a, b):
    return lax.dot_general(a, b, (((1,), (1,)), ((), ())), preferred_element_type=F32)


def _mm(a, b):
    return jnp.dot(a, b, preferred_element_type=F32)


def _split_bf16(x, parts):
    out = []
    r = x
    for i in range(parts):
        p = r.astype(BF16)
        out.append(p)
        if i + 1 < parts:
            r = r - p.astype(F32)
    return out


def _mm_split(x, w, parts=3):
    return sum(_mm(p, w) for p in _split_bf16(x, parts))


def _log_sigmoid(x):
    return jnp.minimum(x, 0.0) - jnp.log1p(jnp.exp(-jnp.abs(x)))


def _softplus(x):
    return jnp.maximum(x, 0.0) + jnp.log(1.0 + jnp.exp(-jnp.abs(x)))


def _sigmoid(x):
    return 1.0 / (1.0 + jnp.exp(-x))


def _rms(x, g):
    return x * lax.rsqrt(jnp.mean(x * x, axis=-1, keepdims=True) + EPS) * g


def _cparams(sem, vmem=None):
    return pltpu.CompilerParams(dimension_semantics=sem, vmem_limit_bytes=vmem)


def _iota(shape, dim):
    return lax.broadcasted_iota(jnp.int32, shape, dim)


def _proj_kernel(x_ref, g_ref, wt_ref, wft_ref, bft_ref,
                 qf_ref, ktfb_ref, ktf_ref, lft_ref, qs_ref, ktsb_ref, kts_ref,
                 qd_ref, kvdb_ref, kvdi_ref, qm_ref):
    hb = _rms(x_ref[...], g_ref[...]).astype(BF16)
    tm = hb.shape[0]
    lane_hi = _iota((tm, LANES), 1) >= HD

    def rows(r0):
        return wt_ref[r0:r0 + 512, :]

    def pad_heads(q, out_ref):
        for h in range(FOX_HEADS):
            grp = q[:, LANES * (h // 2):LANES * (h // 2 + 1)]
            dst_hi = (h // 2) % 2 == 1
            if (h % 2 == 1) != dst_hi:
                grp = pltpu.roll(grp, HD, 1)
            keep = lane_hi if dst_hi else jnp.logical_not(lane_hi)
            out_ref[:, LANES * h:LANES * (h + 1)] = jnp.where(keep, grp, 0.0).astype(BF16)

    pad_heads(_nt(hb, rows(_R_FQ)) * 0.125, qf_ref)
    kt = _nt(rows(_R_FKV), hb)
    ktf_ref[...] = kt
    ktfb_ref[...] = kt.astype(BF16)
    pad_heads(_nt(hb, rows(_R_SQ)) * 0.125, qs_ref)
    kt = _nt(rows(_R_SKV), hb)
    kts_ref[...] = kt
    ktsb_ref[...] = kt.astype(BF16)
    qd_ref[...] = (_nt(hb, rows(_R_DQ)) * 0.125).astype(BF16)
    kv = _nt(hb, rows(_R_DKV))
    kvdb_ref[...] = kv.astype(BF16)
    for c in range(4):
        kvdi_ref[pl.ds(c, tm, stride=4), :] = kv[:, LANES * c:LANES * (c + 1)]
    qm_ref[...] = _nt(hb, rows(_R_MQ)).astype(BF16)
    lft_ref[...] = _log_sigmoid(_nt(wft_ref[...], hb) + bft_ref[...])


def _proj_call(x2, g, wt, wft, bft, B, S, tm):
    T = B * S
    nt = S // tm
    row = lambda n: pl.BlockSpec((tm, n), lambda b, i: (b * nt + i, 0))
    col = lambda n: pl.BlockSpec((None, n, tm), lambda b, i: (b, 0, i))
    full = lambda a: pl.BlockSpec(a.shape, lambda b, i: (0,) * a.ndim)
    sds = jax.ShapeDtypeStruct
    out_shape = (
        sds((T, 1024), BF16), sds((B, 512, S), BF16), sds((B, 512, S), F32),
        sds((B, FOX_HEADS, S), F32),
        sds((T, 1024), BF16), sds((B, 512, S), BF16), sds((B, 512, S), F32),
        sds((T, 512), BF16), sds((T, 512), BF16), sds((4 * T, LANES), F32),
        sds((T, 512), BF16))
    out_specs = (
        row(1024), col(512), col(512), col(FOX_HEADS),
        row(1024), col(512), col(512),
        row(512), row(512), pl.BlockSpec((4 * tm, LANES), lambda b, i: (b * nt + i, 0)),
        row(512))
    return pl.pallas_call(
        _proj_kernel, grid=(B, nt),
        in_specs=[row(D_MODEL), full(g), full(wt), full(wft), full(bft)],
        out_specs=out_specs, out_shape=out_shape,
        compiler_params=_cparams(("parallel", "parallel"), VMEM_LIMIT),
    )(x2, g, wt, wft, bft)


def _memproj_kernel(x_ref, g_ref, w_ref, kvi_ref, kvb_ref):
    hb = _rms(x_ref[...], g_ref[...]).astype(BF16)
    tm = hb.shape[0]
    kv = _mm(hb, w_ref[...])
    kvb_ref[...] = kv.astype(BF16)
    for c in range(2 * MEM_HEADS):
        kvi_ref[pl.ds(c, tm, stride=2 * MEM_HEADS), :] = kv[:, MEM_HD * c:MEM_HD * (c + 1)]


def _memproj_call(x2, g, w, tm):
    T, N = x2.shape[0], w.shape[1]
    return pl.pallas_call(
        _memproj_kernel, grid=(T // tm,),
        in_specs=[pl.BlockSpec((tm, D_MODEL), lambda i: (i, 0)),
                  pl.BlockSpec(g.shape, lambda i: (0, 0)),
                  pl.BlockSpec(w.shape, lambda i: (0, 0))],
        out_specs=(pl.BlockSpec((2 * MEM_HEADS * tm, MEM_HD), lambda i: (i, 0)),
                   pl.BlockSpec((tm, N), lambda i: (i, 0))),
        out_shape=(jax.ShapeDtypeStruct((2 * MEM_HEADS * T, MEM_HD), F32),
                   jax.ShapeDtypeStruct((T, N), BF16)),
        compiler_params=_cparams(("parallel",), VMEM_LIMIT),
    )(x2, g, w)


def _cumsum_kernel(lft_ref, f_ref, *, chunk):
    S = lft_ref.shape[1]
    tri = (_iota((chunk, chunk), 0) <= _iota((chunk, chunk), 1)).astype(BF16)
    carry = jnp.zeros((FOX_HEADS, 1), F32)
    for c in range(S // chunk):
        loc = _mm_split(lft_ref[:, c * chunk:(c + 1) * chunk], tri)
        f_ref[:, c * chunk:(c + 1) * chunk] = loc + carry
        carry = carry + loc[:, chunk - 1:chunk]


def _cumsum_call(lft):
    B, _, S = lft.shape
    spec = pl.BlockSpec((None, FOX_HEADS, S), lambda b: (b, 0, 0))
    return pl.pallas_call(
        functools.partial(_cumsum_kernel, chunk=min(256, S)), grid=(B,),
        in_specs=[spec], out_specs=spec,
        out_shape=jax.ShapeDtypeStruct(lft.shape, F32),
        compiler_params=_cparams(("parallel",)),
    )(lft)


def _t5_bucket(dist):
    max_exact = N_BUCKETS // 2
    d1 = jnp.maximum(dist, 1).astype(F32)
    large = max_exact + (jnp.log(d1 / max_exact) / math.log(MAX_DISTANCE / max_exact)
                         * (N_BUCKETS - max_exact)).astype(jnp.int32)
    large = jnp.minimum(large, N_BUCKETS - 1)
    return jnp.where(dist < max_exact, dist, large)


def _bias_kernel(rb_ref, tiles_ref, dec_ref, *, tile):
    i = _iota((tile, tile), 0)
    j = _iota((tile, tile), 1)
    for d in range(3):
        bkt = _t5_bucket(jnp.maximum(tile * d + i - j, 0))
        for h in range(DIFF_HEADS):
            acc = jnp.zeros((tile, tile), F32)
            for b in range(N_BUCKETS):
                acc = jnp.where(bkt == b, rb_ref[b, h], acc)
            tiles_ref[d, h] = acc
    r = _iota((2 * DIFF_HEADS, LANES), 0) % DIFF_HEADS
    lane = _iota((2 * DIFF_HEADS, LANES), 1)
    for idx, dist in enumerate((PAGE - lane, jnp.full_like(lane, 2 * MAX_DISTANCE),
                                jnp.zeros_like(lane))):
        bkt = _t5_bucket(dist)
        acc = jnp.zeros((2 * DIFF_HEADS, LANES), F32)
        for h in range(DIFF_HEADS):
            for b in range(N_BUCKETS):
                acc = jnp.where((bkt == b) & (r == h), rb_ref[b, h], acc)
        dec_ref[idx] = acc


def _bias_call(rel_bias, tile):
    return pl.pallas_call(
        functools.partial(_bias_kernel, tile=tile),
        in_specs=[pl.BlockSpec(memory_space=pltpu.SMEM)],
        out_specs=(pl.BlockSpec(memory_space=pltpu.VMEM),) * 2,
        out_shape=(jax.ShapeDtypeStruct((3, DIFF_HEADS, tile, tile), F32),
                   jax.ShapeDtypeStruct((3, 2 * DIFF_HEADS, LANES), F32)),
    )(rel_bias)


def _causal_mask(t):
    return _iota((t, t), 1) <= _iota((t, t), 0)


def _softmax_update(s, pv, carry):
    m, l, acc = carry
    m_new = jnp.maximum(m, jnp.max(s, axis=-1, keepdims=True))
    alpha = jnp.exp(m - m_new)
    p = jnp.exp(s - m_new)
    l = alpha * l + jnp.sum(p, axis=-1, keepdims=True)
    acc = alpha * acc + pv(p.astype(BF16))
    return m_new, l, acc


def _softmax_init(t):
    return (jnp.full((t, 1), NEG, F32), jnp.zeros((t, 1), F32), jnp.zeros((t, LANES), F32))


def _fox_kernel(q_ref, kt_ref, vt_ref, fcol_ref, frow_ref, o_ref, *, t):
    qi = pl.program_id(1)
    mask = _causal_mask(t)

    def step(kb, carry, masked):
        ks = pl.multiple_of(kb * t, t)
        out = []
        for h in range(FOX_HEADS):
            lo = LANES * (h // 4)
            s = _mm(q_ref[:, LANES * h:LANES * (h + 1)], kt_ref[lo:lo + LANES, pl.ds(ks, t)])
            s = s + (fcol_ref[:, h:h + 1] - frow_ref[h:h + 1, pl.ds(ks, t)])
            if masked:
                s = jnp.where(mask, s, NEG)
            vt = vt_ref[lo:lo + LANES, pl.ds(ks, t)]
            out.append(_softmax_update(s, lambda p, vt=vt: _nt(p, vt), carry[h]))
        return tuple(out)

    init = tuple(_softmax_init(t) for _ in range(FOX_HEADS))
    carry = lax.fori_loop(0, qi, functools.partial(step, masked=False), init)
    carry = step(qi, carry, True)
    for h in range(FOX_HEADS):
        _, l, acc = carry[h]
        o = acc / l
        half = HD * ((h // 2) % 2)
        o_ref[:, HD * h:HD * (h + 1)] = o[:, half:half + HD]


def _sb_kernel(q_ref, kt_ref, vt_ref, o_ref, *, t):
    qi = pl.program_id(1)
    strict = _iota((t, t), 1) < _iota((t, t), 0)
    tri = (_iota((t, t), 0) > _iota((t, t), 1)).astype(BF16)

    def step(kb, carry, masked):
        ks = pl.multiple_of(kb * t, t)
        out = []
        for h in range(FOX_HEADS):
            run, acc = carry[h]
            lo = LANES * (h // 4)
            z = _mm(q_ref[:, LANES * h:LANES * (h + 1)], kt_ref[lo:lo + LANES, pl.ds(ks, t)])
            sp = _softplus(z)
            l1m = -sp
            if masked:
                l1m = jnp.where(strict, l1m, 0.0)
            rev = _mm_split(l1m, tri, 2)
            w = jnp.exp((z - sp) + rev + run)
            if masked:
                w = jnp.where(strict, w, 0.0)
            acc = acc + _nt(w.astype(BF16), vt_ref[lo:lo + LANES, pl.ds(ks, t)])
            run = run + rev[:, 0:1] + l1m[:, 0:1]
            out.append((run, acc))
        return tuple(out)

    def alive(carry):
        top = functools.reduce(jnp.maximum, [c[0] for c in carry])
        return jnp.max(top) > SB_DEAD

    init = tuple((jnp.zeros((t, 1), F32), jnp.zeros((t, LANES), F32)) for _ in range(FOX_HEADS))
    carry = step(qi, init, True)

    def body(state):
        kb, _, carry = state
        carry = step(kb, carry, False)
        return kb - 1, alive(carry), carry

    _, _, carry = lax.while_loop(lambda st: jnp.logical_and(st[0] >= 0, st[1]), body,
                                 (qi - 1, alive(carry), carry))
    for h in range(FOX_HEADS):
        half = HD * ((h // 2) % 2)
        o_ref[:, HD * h:HD * (h + 1)] = carry[h][1][:, half:half + HD]


def _diff_lambda(lam_ref, lam_init):
    lf = lam_ref[...]
    a = jnp.sum(lf[0:1] * lf[1:2], axis=-1, keepdims=True)
    b = jnp.sum(lf[2:3] * lf[3:4], axis=-1, keepdims=True)
    return jnp.exp(a) - jnp.exp(b) + lam_init


def _diff_kernel(q_ref, k_ref, v_ref, bias_ref, lam_ref, gsub_ref, o_ref, *, t, lam_init):
    qi = pl.program_id(1)
    mask = _causal_mask(t)
    lane_hi = _iota((t, LANES), 1) >= HD

    def step(kb, carry, masked):
        ks = pl.multiple_of(kb * t, t)
        d = jnp.minimum(qi - kb, 2)
        out = []
        for h in range(DIFF_HEADS):
            lo = LANES * (h // 2)
            q = q_ref[:, LANES * h:LANES * (h + 1)]
            zero = jnp.zeros_like(q)
            k = k_ref[pl.ds(ks, t), lo:lo + LANES]
            v = v_ref[pl.ds(ks, t), lo:lo + LANES]
            bias = bias_ref[d, h]
            for mp in range(2):
                qm = jnp.where(lane_hi, q, zero) if mp else jnp.where(lane_hi, zero, q)
                s = _nt(qm, k) + bias
                if masked:
                    s = jnp.where(mask, s, NEG)
                out.append(_softmax_update(s, lambda p, v=v: _mm(p, v), carry[2 * h + mp]))
        return tuple(out)

    init = tuple(_softmax_init(t) for _ in range(2 * DIFF_HEADS))
    carry = lax.fori_loop(0, qi, functools.partial(step, masked=False), init)
    carry = step(qi, carry, True)
    lam = _diff_lambda(lam_ref, lam_init)
    for h in range(DIFF_HEADS):
        (_, l0, a0), (_, l1, a1) = carry[2 * h], carry[2 * h + 1]
        o = a0 / l0 - lam * (a1 / l1)
        o_ref[:, LANES * h:LANES * (h + 1)] = _rms(o, gsub_ref[...]) * (1.0 - lam_init)


def _mem_kernel(q_ref, k_ref, v_ref, o_ref):
    for h in range(MEM_HEADS):
        sl = slice(MEM_HD * h, MEM_HD * (h + 1))
        s = _nt(q_ref[:, sl], k_ref[:, sl]) * MEM_HD ** -0.5
        e = jnp.exp(s - jnp.max(s, axis=-1, keepdims=True))
        p = e / jnp.sum(e, axis=-1, keepdims=True)
        o_ref[:, sl] = _mm(p.astype(BF16), v_ref[:, sl])


def _attn_call(body, args, in_specs, B, S, t):
    return pl.pallas_call(
        body, grid=(B, S // t), in_specs=in_specs,
        out_specs=pl.BlockSpec((None, t, BRANCH_W), lambda b, i: (b, i, 0)),
        out_shape=jax.ShapeDtypeStruct((B, S, BRANCH_W), F32),
        compiler_params=_cparams(("parallel", "arbitrary"), VMEM_LIMIT),
    )(*args)


def _q_spec(t, w):
    return pl.BlockSpec((None, t, w), lambda b, i: (b, i, 0))


def _kt_specs(S):
    return [pl.BlockSpec((None, 256, S), lambda b, i: (b, 0, 0)),
            pl.BlockSpec((None, 256, S), lambda b, i: (b, 1, 0))]


def _kv_specs(n, w):
    return [pl.BlockSpec((None, n, w), lambda b, i: (b, 0, 0)),
            pl.BlockSpec((None, n, w), lambda b, i: (b, 0, 1))]


def _const_spec(a):
    return pl.BlockSpec(a.shape, lambda b, i: (0,) * a.ndim)


def _merge_kernel(x_ref, gpre_ref, gpost_ref, of_ref, os_ref, od_ref, om_ref,
                  wgpt_ref, wmgt_ref, wbr_ref, wout_ref, y_ref):
    x = x_ref[...]
    hb = _rms(x, gpre_ref[...]).astype(BF16)
    y = jnp.zeros(x.shape, F32)
    for n, o_ref in enumerate((of_ref, os_ref, od_ref, om_ref)):
        gp = _nt(hb, wgpt_ref[BRANCH_W * n:BRANCH_W * (n + 1), :])
        o = o_ref[...] * (gp * _sigmoid(gp))
        u = _mm(o.astype(BF16), wbr_ref[n])
        mg = _nt(hb, wmgt_ref[D_MODEL * n:D_MODEL * (n + 1), :])
        y = y + _sigmoid(mg) * u
    out = _mm(y.astype(BF16), wout_ref[...])
    y_ref[...] = x + _rms(out, gpost_ref[...])


def _merge_call(x2, gpre, gpost, outs, wgpt, wmgt, wbr, wout, tm):
    T = x2.shape[0]
    row = lambda n: pl.BlockSpec((tm, n), lambda i: (i, 0))

    def const(a):
        return pl.BlockSpec(a.shape, lambda i: (0,) * a.ndim, pipeline_mode=pl.Buffered(1))

    return pl.pallas_call(
        _merge_kernel, grid=(T // tm,),
        in_specs=[row(D_MODEL), const(gpre), const(gpost)] + [row(BRANCH_W)] * 4
        + [const(wgpt), const(wmgt), const(wbr), const(wout)],
        out_specs=row(D_MODEL),
        out_shape=jax.ShapeDtypeStruct((T, D_MODEL), F32),
        compiler_params=_cparams(("parallel",), VMEM_LIMIT),
    )(x2, gpre, gpost, *outs, wgpt, wmgt, wbr, wout)


def _suffix_terms(x, run, pps):
    R = 8 * pps
    tri = (_iota((LANES, LANES), 0) > _iota((LANES, LANES), 1)).astype(BF16)
    ones = jnp.ones((LANES, LANES), BF16)
    r0, r1 = _iota((R, R), 0), _iota((R, R), 1)
    newer = ((r1 % 8 == r0 % 8) & (r1 < r0)).astype(BF16)
    parts = _split_bf16(x, 3)
    within = sum(_mm(p, tri) for p in parts)
    total = sum(_mm(p, ones) for p in parts)
    cross = sum(_mm(newer, p) for p in _split_bf16(total, 3))
    run_rows = jnp.concatenate([run] * pps, axis=0)
    last = slice(R - 8, R)
    new_run = run + cross[last, 0:1] + total[last, 0:1]
    return within + cross + run_rows, new_run


def _group_max(x, pps):
    return functools.reduce(jnp.maximum, [x[8 * i:8 * (i + 1)] for i in range(pps)])


def _group_sum(x, pps):
    return functools.reduce(jnp.add, [x[8 * i:8 * (i + 1)] for i in range(pps)])


def _decode_kernel(pt_ref, *refs, kind, pps, lam_init):
    del pt_ref
    it = iter(refs)
    q_ref, kvnew_ref = next(it), next(it)
    pages = [next(it) for _ in range(pps)]
    if kind == "fox":
        lfnew_ref = next(it)
        lfs = [next(it) for _ in range(pps)]
    if kind == "diff":
        dbias_ref, lam_ref, gsub_ref = next(it), next(it), next(it)
    o_ref = next(it)
    m_ref, l_ref, acc_ref, run_ref = next(it), next(it), next(it), next(it)
    g = pl.program_id(1)
    q = q_ref[...]

    @pl.when(g == 0)
    def _():
        if kind == "sb":
            run_ref[...] = jnp.zeros_like(run_ref)
            acc_ref[...] = jnp.zeros_like(acc_ref)
        else:
            knew = kvnew_ref[:, 0:256].astype(BF16).astype(F32)
            vnew = kvnew_ref[:, 256:512].astype(BF16).astype(F32)
            s = jnp.sum(q.astype(F32) * knew, axis=-1, keepdims=True)
            if kind == "diff":
                s = s + dbias_ref[2][:, 0:1]
            m_ref[...] = s
            l_ref[...] = jnp.ones_like(l_ref)
            acc_ref[...] = jnp.broadcast_to(vnew, acc_ref.shape)
            if kind == "fox":
                run_ref[...] = lfnew_ref[...]

    if kind == "diff":
        def keys(page, c):
            return page[pl.ds(c, PAGE, stride=4), :].astype(BF16)
        s = jnp.concatenate(
            [_nt(q[:, 0:LANES], keys(p, 0)) + _nt(q[:, LANES:2 * LANES], keys(p, 1)) for p in pages],
            axis=0)

        def pv(w, page):
            return jnp.concatenate([_mm(w, keys(page, 2)), _mm(w, keys(page, 3))], axis=1)
    else:
        s = jnp.concatenate([_mm(q, p[0:256, :].astype(BF16)) for p in pages], axis=0)

        def pv(w, page):
            return _nt(w, page[256:512, :].astype(BF16))

    def weighted_values(w):
        wb = w.astype(BF16)
        return sum(pv(wb[8 * i:8 * (i + 1)], pages[i]) for i in range(pps))

    if kind == "sb":
        sp = _softplus(s)
        rev, new_run = _suffix_terms(-sp, run_ref[...], pps)
        acc_ref[...] = acc_ref[...] + weighted_values(jnp.exp((s - sp) + rev))
        run_ref[...] = new_run
    else:
        if kind == "fox":
            decay, new_run = _suffix_terms(
                jnp.concatenate([lf[...] for lf in lfs], axis=0), run_ref[...], pps)
            s = s + decay
            run_ref[...] = new_run
        else:
            first = jnp.where(g == 0, dbias_ref[0], dbias_ref[1])
            s = s + jnp.concatenate([first] + [dbias_ref[1]] * (pps - 1), axis=0)
        m = m_ref[...]
        m_new = jnp.maximum(m, _group_max(jnp.max(s, axis=-1, keepdims=True), pps))
        alpha = jnp.exp(m - m_new)
        p = jnp.exp(s - jnp.concatenate([m_new] * pps, axis=0))
        l_ref[...] = alpha * l_ref[...] + _group_sum(jnp.sum(p, axis=-1, keepdims=True), pps)
        acc_ref[...] = alpha * acc_ref[...] + weighted_values(p)
        m_ref[...] = m_new

    @pl.when(g == pl.num_programs(1) - 1)
    def _():
        if kind == "diff":
            n = acc_ref[...] / l_ref[...]
            lam = _diff_lambda(lam_ref, lam_init)
            o = n[0:DIFF_HEADS] - lam * n[DIFF_HEADS:2 * DIFF_HEADS]
            row = _iota((DIFF_HEADS, LANES), 0)
            o = jnp.where(row // 2 == 0, o[:, 0:LANES], o[:, LANES:2 * LANES])
            o_ref[...] = _rms(o, gsub_ref[...]) * (1.0 - lam_init)
        else:
            o = acc_ref[...] if kind == "sb" else acc_ref[...] / l_ref[...]
            row = _iota((FOX_HEADS, HD), 0)
            res = o[:, 0:HD]
            for kvh in range(1, 4):
                res = jnp.where(row // 2 == kvh, o[:, HD * kvh:HD * (kvh + 1)], res)
            o_ref[...] = res


def _decode_call(kind, layer, pt, qbd, kvnew, pool, extra, lam_init, pps):
    B, n_pages = pt.shape
    steps = n_pages // pps
    oshape = (DIFF_HEADS, LANES) if kind == "diff" else (FOX_HEADS, HD)

    def page_map(i):
        return lambda b, g, pt: (layer, pt[b, n_pages - 1 - (g * pps + i)], 0, 0)

    per_b = lambda a: pl.BlockSpec((None,) + a.shape[1:], lambda b, g, pt: (b,) + (0,) * (a.ndim - 1))
    const = lambda a: pl.BlockSpec(a.shape, lambda b, g, pt: (0,) * a.ndim)
    in_specs = [per_b(qbd), per_b(kvnew)]
    in_specs += [pl.BlockSpec((None, None, 512, LANES), page_map(i)) for i in range(pps)]
    args = [qbd, kvnew] + [pool] * pps
    if kind == "fox":
        lfnew, lfpool = extra
        in_specs += [per_b(lfnew)]
        in_specs += [pl.BlockSpec((None, None, FOX_HEADS, PAGE), page_map(i)) for i in range(pps)]
        args += [lfnew] + [lfpool] * pps
    elif kind == "diff":
        in_specs += [const(a) for a in extra]
        args += list(extra)
    grid_spec = pltpu.PrefetchScalarGridSpec(
        num_scalar_prefetch=1, grid=(B, steps), in_specs=in_specs,
        out_specs=pl.BlockSpec((None,) + oshape, lambda b, g, pt: (b, 0, 0)),
        scratch_shapes=[pltpu.VMEM((8, 1), F32), pltpu.VMEM((8, 1), F32),
                        pltpu.VMEM((8, 256), F32), pltpu.VMEM((8, 1), F32)])
    return pl.pallas_call(
        functools.partial(_decode_kernel, kind=kind, pps=pps, lam_init=lam_init),
        grid_spec=grid_spec,
        out_shape=jax.ShapeDtypeStruct((B,) + oshape, F32),
        compiler_params=_cparams(("parallel", "arbitrary"), VMEM_LIMIT),
    )(pt, *args)


def _memdec_kernel(q_ref, kv_ref, o_ref):
    n = kv_ref.shape[0] // (2 * MEM_HEADS)

    def rows(c):
        return kv_ref[pl.ds(c, n, stride=2 * MEM_HEADS), :].astype(BF16)

    q = q_ref[...]
    s = sum(_nt(q[:, MEM_HD * h:MEM_HD * (h + 1)], rows(h)) for h in range(MEM_HEADS))
    s = s * MEM_HD ** -0.5
    e = jnp.exp(s - jnp.max(s, axis=-1, keepdims=True))
    p = (e / jnp.sum(e, axis=-1, keepdims=True)).astype(BF16)
    row = _iota((8, MEM_HD), 0)
    res = jnp.zeros((8, MEM_HD), F32)
    for h in range(MEM_HEADS):
        res = jnp.where(row == h, _mm(p, rows(MEM_HEADS + h)), res)
    o_ref[...] = res[0:MEM_HEADS]


def _memdec_call(layer, qbd, cache):
    B = qbd.shape[0]
    return pl.pallas_call(
        _memdec_kernel, grid=(B,),
        in_specs=[pl.BlockSpec((None, 8, 512), lambda b: (b, 0, 0)),
                  pl.BlockSpec((None, None) + cache.shape[2:], lambda b: (layer, b, 0, 0))],
        out_specs=pl.BlockSpec((None, MEM_HEADS, MEM_HD), lambda b: (b, 0, 0)),
        out_shape=jax.ShapeDtypeStruct((B, MEM_HEADS, MEM_HD), F32),
        compiler_params=_cparams(("parallel",), VMEM_LIMIT),
    )(qbd, cache)


def _pick(n, pref):
    t = min(n, pref)
    while n % t:
        t //= 2
    return t


def _layer_weights(w_in_l, b_f_l, w_branch_l, w_out_l):
    c = (0, 512, 768, 1024, 1032, 1544, 1800, 2056, 2568, 2824, 3080, 3592, 5640, 9736)
    wt = jnp.transpose(w_in_l)
    return dict(
        wt=jnp.concatenate([wt[c[0]:c[3]], wt[c[4]:c[11]]], axis=0).astype(BF16),
        wft=wt[c[3]:c[4]].astype(BF16), bft=b_f_l.reshape(FOX_HEADS, 1),
        wgpt=wt[c[11]:c[12]].astype(BF16), wmgt=wt[c[12]:c[13]].astype(BF16),
        wbr=w_branch_l.astype(BF16), wout=w_out_l.astype(BF16))


def kernel(x_prompt, x_sample, cache_fox_kv, cache_fox_logf, cache_sb_kv, cache_diff_kv, cache_mem_kv,
           page_table, mem_prompt, rel_bias, g_pre, g_post, g_mem, w_in, b_f, w_mem_kv, lam,
           g_diff_sub, w_branch, w_out):
    B, S, D = x_prompt.shape
    DB = x_sample.shape[0]
    depth = w_in.shape[0]
    n_pool = cache_fox_kv.shape[1]
    n_pages = page_table.shape[1]
    mem_len = mem_prompt.shape[1]
    T = B * S
    t_att = _pick(S, 256)
    tm_proj = _pick(S, 512)
    tm_merge = _pick(T, 256)
    pps = _pick(n_pages, 16)

    feat_major = lambda c: jnp.transpose(c, (0, 1, 3, 4, 5, 2)).reshape(depth, n_pool, 512, PAGE)
    fox_pool = feat_major(cache_fox_kv)
    sb_pool = feat_major(cache_sb_kv)
    diff_pool = cache_diff_kv.reshape(depth, n_pool, 4 * PAGE, LANES)
    lf_pool = jnp.swapaxes(cache_fox_logf, 2, 3)
    mem_cache = cache_mem_kv.reshape(depth, DB, 2 * MEM_HEADS * mem_len, MEM_HD)
    bias_tiles, dec_bias = _bias_call(rel_bias, t_att)

    h8 = jnp.arange(8)
    sel_pair = (jnp.arange(2)[None, :] == (h8 // 4)[:, None]).astype(BF16)
    r8m, r8h = h8 // 4, h8 % 4
    sel_diff = ((jnp.arange(2)[None, :, None] == (r8h // 2)[:, None, None])
                & (jnp.arange(2)[None, None, :] == r8m[:, None, None])).astype(BF16)
    sel_mem = (jnp.arange(8)[:, None] == jnp.arange(4)[None, :]).astype(BF16)

    xp = x_prompt.reshape(T, D)
    xs = x_sample.reshape(DB, D)
    mem2 = mem_prompt.reshape(B * mem_len, D)
    st_p = [[] for _ in range(5)]
    st_s = [[] for _ in range(4)]
    for l in range(depth):
        lam_init = 0.8 - 0.6 * math.exp(-0.3 * l)
        w = _layer_weights(w_in[l], b_f[l], w_branch[l], w_out[l])
        gpre, gpost = g_pre[l].reshape(1, D), g_post[l].reshape(1, D)
        gsub = g_diff_sub[l].reshape(1, 2 * HD)

        (qf, ktfb, ktf, lft, qs, ktsb, kts, qd, kvdb, kvdi, qm) = _proj_call(
            xp, gpre, w["wt"], w["wft"], w["bft"], B, S, tm_proj)
        mkvi, mkvb = _memproj_call(mem2, g_mem[l].reshape(1, D), w_mem_kv[l].astype(BF16),
                                   _pick(B * mem_len, 512))
        f_row = _cumsum_call(lft)
        f_col = jnp.swapaxes(f_row, 1, 2)
        r3 = lambda a: a.reshape(B, S, a.shape[-1])
        o_fox = _attn_call(
            functools.partial(_fox_kernel, t=t_att), [r3(qf), ktfb, ktfb, f_col, f_row],
            [_q_spec(t_att, 1024)] + _kt_specs(S)
            + [pl.BlockSpec((None, t_att, FOX_HEADS), lambda b, i: (b, i, 0)),
               pl.BlockSpec((None, FOX_HEADS, S), lambda b, i: (b, 0, 0))], B, S, t_att)
        o_sb = _attn_call(
            functools.partial(_sb_kernel, t=t_att), [r3(qs), ktsb, ktsb],
            [_q_spec(t_att, 1024)] + _kt_specs(S), B, S, t_att)
        kvd3 = r3(kvdb)
        o_diff = _attn_call(
            functools.partial(_diff_kernel, t=t_att, lam_init=lam_init),
            [r3(qd), kvd3, kvd3, bias_tiles, lam[l], gsub],
            [_q_spec(t_att, 512)] + _kv_specs(S, 256)
            + [_const_spec(bias_tiles), _const_spec(lam[l]), _const_spec(gsub)], B, S, t_att)
        mkv3 = mkvb.reshape(B, mem_len, 1024)
        o_mem = _attn_call(_mem_kernel, [r3(qm), mkv3, mkv3],
                           [_q_spec(t_att, 512)] + _kv_specs(mem_len, 512), B, S, t_att)
        flat = lambda a: a.reshape(T, BRANCH_W)
        xp = _merge_call(xp, gpre, gpost, [flat(o_fox), flat(o_sb), flat(o_diff), flat(o_mem)],
                         w["wgpt"], w["wmgt"], w["wbr"], w["wout"], tm_merge)
        tok_major = lambda kt: jnp.transpose(kt.reshape(B, 2, 4, HD, S), (0, 4, 1, 2, 3))
        st_p[0].append(tok_major(ktf))
        st_p[1].append(jnp.swapaxes(lft, 1, 2))
        st_p[2].append(tok_major(kts))
        st_p[3].append(kvdi.reshape(B, S, 2, 2, 2 * HD))
        st_p[4].append(mkvi.reshape(B, mem_len, 2, MEM_HEADS, MEM_HD))

        (qf, _, ktf, lft, qs, _, kts, qd, _, kvdi, qm) = _proj_call(
            xs, gpre, w["wt"], w["wft"], w["bft"], 1, DB, DB)
        kvf, kvs = jnp.transpose(ktf[0]), jnp.transpose(kts[0])
        kvd = kvdi.reshape(DB, 512)
        lfs = jnp.transpose(lft[0])
        place = lambda q: (q.reshape(DB, 8, 1, LANES) * sel_pair[None, :, :, None]).reshape(DB, 8, 256)
        qd_rows = jnp.tile(qd.reshape(DB, 1, DIFF_HEADS, 1, 2, HD), (1, 2, 1, 1, 1, 1)).reshape(DB, 8, 1, 2, HD)
        qbd_diff = (qd_rows * sel_diff[None, :, :, :, None]).reshape(DB, 8, 256)
        qbd_mem = (qm.reshape(DB, 1, MEM_HEADS, MEM_HD)
                   * sel_mem[None, :, :, None]).reshape(DB, 8, 512)
        new = lambda kv: kv.reshape(DB, 1, 512)
        a_fox = _decode_call("fox", l, page_table, place(qf), new(kvf), fox_pool,
                             (lfs.reshape(DB, FOX_HEADS, 1), lf_pool), lam_init, pps)
        a_sb = _decode_call("sb", l, page_table, place(qs), new(kvs), sb_pool, (), lam_init, pps)
        a_diff = _decode_call("diff", l, page_table, qbd_diff, new(kvd), diff_pool,
                              (dec_bias, lam[l], gsub), lam_init, pps)
        a_mem = _memdec_call(l, qbd_mem, mem_cache)
        flat_s = lambda a: a.reshape(DB, BRANCH_W)
        xs = _merge_call(xs, gpre, gpost, [flat_s(a_fox), flat_s(a_sb), flat_s(a_diff), flat_s(a_mem)],
                         w["wgpt"], w["wmgt"], w["wbr"], w["wout"], DB)
        st_s[0].append(kvf.reshape(DB, 1, 2, 4, HD))
        st_s[1].append(lfs.reshape(DB, 1, FOX_HEADS))
        st_s[2].append(kvs.reshape(DB, 1, 2, 4, HD))
        st_s[3].append(kvd.reshape(DB, 1, 2, 2, 2 * HD))

    return (xp.reshape(B, S, D), xs.reshape(DB, 1, D),
            *[jnp.stack(s, 0) for s in st_p], *[jnp.stack(s, 0) for s in st_s])
```

```python
import functools
import math

import jax
import jax.numpy as jnp
from jax import lax
from jax.experimental import pallas as pl
from jax.experimental.pallas import tpu as pltpu

F32 = jnp.float32
BF16 = jnp.bfloat16

D_MODEL = 1024
N_BRANCH = 4
BRANCH_W = D_MODEL // 2
FOX_HEADS = 8
HD = 64
DIFF_HEADS = 4
MEM_HEADS = 4
MEM_HD = 128
N_BUCKETS = 32
MAX_DISTANCE = 128
EPS = 1e-6
NEG = -1e30
PAGE = 128
LANES = 128
VMEM_LIMIT = 56 * 1024 * 1024
SB_DEAD = -104.0

_R_FQ, _R_FKV, _R_SQ, _R_SKV, _R_DQ, _R_DKV, _R_MQ = 0, 512, 1024, 1536, 2048, 2560, 3072


def _nt(a, b):
    return lax.dot_general(a, b, (((1,), (1,)), ((), ())), preferred_element_type=F32)


def _tn(a, b):
    return lax.dot_general(a, b, (((0,), (0,)), ((), ())), preferred_element_type=F32)


def _mm(a, b):
    return jnp.dot(a, b, preferred_element_type=F32)


def _split_bf16(x, parts):
    out = []
    r = x
    for i in range(parts):
        p = r.astype(BF16)
        out.append(p)
        if i + 1 < parts:
            r = r - p.astype(F32)
    return out


def _mm_split(x, w, parts=3):
    return sum(_mm(p, w) for p in _split_bf16(x, parts))


def _log_sigmoid(x):
    return jnp.minimum(x, 0.0) - jnp.log1p(jnp.exp(-jnp.abs(x)))


def _softplus(x):
    return jnp.maximum(x, 0.0) + jnp.log(1.0 + jnp.exp(-jnp.abs(x)))


def _sigmoid(x):
    return 1.0 / (1.0 + jnp.exp(-x))


def _rms(x, g):
    return x * lax.rsqrt(jnp.mean(x * x, axis=-1, keepdims=True) + EPS) * g


def _cparams(sem, vmem=None):
    return pltpu.CompilerParams(dimension_semantics=sem, vmem_limit_bytes=vmem)


def _iota(shape, dim):
    return lax.broadcasted_iota(jnp.int32, shape, dim)


def _proj_kernel(x_ref, g_ref, wt_ref, wft_ref, bft_ref,
                 qft_ref, ktfb_ref, ktf_ref, lft_ref, qst_ref, ktsb_ref, kts_ref,
                 qdt_ref, kdb_ref, vdtb_ref, kvdi_ref, qm_ref):
    hb = _rms(x_ref[...], g_ref[...]).astype(BF16)
    tm = hb.shape[0]

    def rows(r0, n=512):
        return wt_ref[r0:r0 + n, :]

    qft_ref[...] = (_nt(rows(_R_FQ), hb) * 0.125).astype(BF16)
    kt = _nt(rows(_R_FKV), hb)
    ktf_ref[...] = kt
    ktfb_ref[...] = kt.astype(BF16)
    qst_ref[...] = (_nt(rows(_R_SQ), hb) * 0.125).astype(BF16)
    kt = _nt(rows(_R_SKV), hb)
    kts_ref[...] = kt
    ktsb_ref[...] = kt.astype(BF16)
    qdt_ref[...] = (_nt(rows(_R_DQ), hb) * 0.125).astype(BF16)
    vdtb_ref[...] = _nt(rows(_R_DKV + 256, 256), hb).astype(BF16)
    kv = _nt(hb, rows(_R_DKV))
    kdb_ref[...] = kv[:, 0:256].astype(BF16)
    for c in range(4):
        kvdi_ref[pl.ds(c, tm, stride=4), :] = kv[:, LANES * c:LANES * (c + 1)]
    qm_ref[...] = _nt(hb, rows(_R_MQ)).astype(BF16)
    lft_ref[...] = _log_sigmoid(_nt(wft_ref[...], hb) + bft_ref[...])


def _proj_call(x2, g, wt, wft, bft, B, S, tm):
    T = B * S
    nt = S // tm
    row = lambda n: pl.BlockSpec((tm, n), lambda b, i: (b * nt + i, 0))
    col = lambda n: pl.BlockSpec((None, n, tm), lambda b, i: (b, 0, i))
    full = lambda a: pl.BlockSpec(a.shape, lambda b, i: (0,) * a.ndim)
    sds = jax.ShapeDtypeStruct
    out_shape = (
        sds((B, 512, S), BF16), sds((B, 512, S), BF16), sds((B, 512, S), F32),
        sds((B, FOX_HEADS, S), F32),
        sds((B, 512, S), BF16), sds((B, 512, S), BF16), sds((B, 512, S), F32),
        sds((B, 512, S), BF16), sds((T, 256), BF16), sds((B, 256, S), BF16),
        sds((4 * T, LANES), F32), sds((T, 512), BF16))
    out_specs = (
        col(512), col(512), col(512), col(FOX_HEADS),
        col(512), col(512), col(512),
        col(512), row(256), col(256),
        pl.BlockSpec((4 * tm, LANES), lambda b, i: (b * nt + i, 0)), row(512))
    return pl.pallas_call(
        _proj_kernel, grid=(B, nt),
        in_specs=[row(D_MODEL), full(g), full(wt), full(wft), full(bft)],
        out_specs=out_specs, out_shape=out_shape,
        compiler_params=_cparams(("parallel", "parallel"), VMEM_LIMIT),
    )(x2, g, wt, wft, bft)


def _memproj_kernel(x_ref, g_ref, w_ref, kvi_ref, kvb_ref):
    hb = _rms(x_ref[...], g_ref[...]).astype(BF16)
    tm = hb.shape[0]
    kv = _mm(hb, w_ref[...])
    kvb_ref[...] = kv.astype(BF16)
    for c in range(2 * MEM_HEADS):
        kvi_ref[pl.ds(c, tm, stride=2 * MEM_HEADS), :] = kv[:, MEM_HD * c:MEM_HD * (c + 1)]


def _memproj_call(x2, g, w, tm):
    T, N = x2.shape[0], w.shape[1]
    return pl.pallas_call(
        _memproj_kernel, grid=(T // tm,),
        in_specs=[pl.BlockSpec((tm, D_MODEL), lambda i: (i, 0)),
                  pl.BlockSpec(g.shape, lambda i: (0, 0)),
                  pl.BlockSpec(w.shape, lambda i: (0, 0))],
        out_specs=(pl.BlockSpec((2 * MEM_HEADS * tm, MEM_HD), lambda i: (i, 0)),
                   pl.BlockSpec((tm, N), lambda i: (i, 0))),
        out_shape=(jax.ShapeDtypeStruct((2 * MEM_HEADS * T, MEM_HD), F32),
                   jax.ShapeDtypeStruct((T, N), BF16)),
        compiler_params=_cparams(("parallel",), VMEM_LIMIT),
    )(x2, g, w)


AUG = 16


def _aug_rows(f2, width, lead):
    row = _iota((AUG, width), 0)
    out = jnp.zeros((AUG, width), F32)
    if lead is None:
        out = jnp.where((row >= 6) & (row < 9), 1.0, out)
        srcs = [(3 * g + i, p) for g in range(2) for i, p in enumerate(_split_bf16(f2[g:g + 1], 3))]
    else:
        out = jnp.where(row // 3 == lead, -1.0, out)
        srcs = [(6 + i, p) for i, p in enumerate(_split_bf16(f2[lead:lead + 1], 3))]
    for r, p in srcs:
        out = jnp.where(row == r, jnp.broadcast_to(p.astype(F32), (AUG, width)), out)
    return out.astype(BF16)


def _cumsum_kernel(lft_ref, f_ref, fa_ref, *, chunk):
    S = lft_ref.shape[1]
    tri = (_iota((chunk, chunk), 0) <= _iota((chunk, chunk), 1)).astype(BF16)
    carry = jnp.zeros((FOX_HEADS, 1), F32)
    for c in range(S // chunk):
        sl = slice(c * chunk, (c + 1) * chunk)
        loc = _mm_split(lft_ref[:, sl], tri)
        f = loc + carry
        f_ref[:, sl] = f
        for kvh in range(FOX_HEADS // 2):
            fa_ref[AUG * kvh:AUG * (kvh + 1), sl] = _aug_rows(f[2 * kvh:2 * kvh + 2], chunk, None)
        carry = carry + loc[:, chunk - 1:chunk]


def _cumsum_call(lft):
    B, _, S = lft.shape
    spec = pl.BlockSpec((None, FOX_HEADS, S), lambda b: (b, 0, 0))
    na = AUG * FOX_HEADS // 2
    return pl.pallas_call(
        functools.partial(_cumsum_kernel, chunk=min(256, S)), grid=(B,),
        in_specs=[spec], out_specs=(spec, pl.BlockSpec((None, na, S), lambda b: (b, 0, 0))),
        out_shape=(jax.ShapeDtypeStruct(lft.shape, F32), jax.ShapeDtypeStruct((B, na, S), BF16)),
        compiler_params=_cparams(("parallel",)),
    )(lft)


def _t5_bucket(dist):
    max_exact = N_BUCKETS // 2
    d1 = jnp.maximum(dist, 1).astype(F32)
    large = max_exact + (jnp.log(d1 / max_exact) / math.log(MAX_DISTANCE / max_exact)
                         * (N_BUCKETS - max_exact)).astype(jnp.int32)
    large = jnp.minimum(large, N_BUCKETS - 1)
    return jnp.where(dist < max_exact, dist, large)


def _bias_kernel(rb_ref, tiles_ref, dec_ref, *, tile):
    key = _iota((tile, tile), 0)
    qry = _iota((tile, tile), 1)
    for d in range(3):
        bkt = _t5_bucket(jnp.maximum(tile * d + qry - key, 0))
        for h in range(DIFF_HEADS):
            acc = jnp.zeros((tile, tile), F32)
            for b in range(N_BUCKETS):
                acc = jnp.where(bkt == b, rb_ref[b, h], acc)
            tiles_ref[d, h] = acc
    r = _iota((2 * DIFF_HEADS, LANES), 0) % DIFF_HEADS
    lane = _iota((2 * DIFF_HEADS, LANES), 1)
    for idx, dist in enumerate((PAGE - lane, jnp.full_like(lane, 2 * MAX_DISTANCE),
                                jnp.zeros_like(lane))):
        bkt = _t5_bucket(dist)
        acc = jnp.zeros((2 * DIFF_HEADS, LANES), F32)
        for h in range(DIFF_HEADS):
            for b in range(N_BUCKETS):
                acc = jnp.where((bkt == b) & (r == h), rb_ref[b, h], acc)
        dec_ref[idx] = acc


def _bias_call(rel_bias, tile):
    return pl.pallas_call(
        functools.partial(_bias_kernel, tile=tile),
        in_specs=[pl.BlockSpec(memory_space=pltpu.SMEM)],
        out_specs=(pl.BlockSpec(memory_space=pltpu.VMEM),) * 2,
        out_shape=(jax.ShapeDtypeStruct((3, DIFF_HEADS, tile, tile), F32),
                   jax.ShapeDtypeStruct((3, 2 * DIFF_HEADS, LANES), F32)),
    )(rel_bias)


def _causal_mask(t):
    return _iota((t, t), 0) <= _iota((t, t), 1)


def _softmax_update(s, vt, carry):
    m, l, acc = carry
    m_new = jnp.maximum(m, jnp.max(s, axis=0, keepdims=True))
    alpha = jnp.exp(m - m_new)
    p = jnp.exp(s - m_new)
    l = alpha * l + jnp.sum(p, axis=0, keepdims=True)
    acc = alpha * acc + _mm(vt, p.astype(BF16))
    return m_new, l, acc


def _softmax_init(t, rows):
    return (jnp.full((1, t), NEG, F32), jnp.zeros((1, t), F32), jnp.zeros((rows, t), F32))


def _store_heads(o_ref, outs):
    for j in range(len(outs) // 2):
        pair = jnp.concatenate([outs[2 * j], outs[2 * j + 1]], axis=0)
        o_ref[:, LANES * j:LANES * (j + 1)] = pair.T


def _fox_kernel(qt_ref, kt_ref, vt_ref, fa_ref, frow_ref, o_ref, *, t):
    qi = pl.program_id(1)
    q0 = pl.multiple_of(qi * t, t)
    mask = _causal_mask(t)
    qa = [jnp.concatenate(
        [qt_ref[HD * h:HD * (h + 1), :],
         _aug_rows(frow_ref[2 * (h // 2):2 * (h // 2) + 2, pl.ds(q0, t)], t, h % 2)], axis=0)
        for h in range(FOX_HEADS)]

    def step(kb, carry, masked):
        ks = pl.multiple_of(kb * t, t)
        out = []
        for h in range(FOX_HEADS):
            kvh = h // 2
            ka = jnp.concatenate([kt_ref[HD * kvh:HD * (kvh + 1), pl.ds(ks, t)],
                                  fa_ref[AUG * kvh:AUG * (kvh + 1), pl.ds(ks, t)]], axis=0)
            s = _tn(ka, qa[h])
            if masked:
                s = jnp.where(mask, s, NEG)
            out.append(_softmax_update(s, vt_ref[HD * kvh:HD * (kvh + 1), pl.ds(ks, t)], carry[h]))
        return tuple(out)

    init = tuple(_softmax_init(t, HD) for _ in range(FOX_HEADS))
    carry = lax.fori_loop(0, qi, functools.partial(step, masked=False), init)
    carry = step(qi, carry, True)
    _store_heads(o_ref, [acc / l for _, l, acc in carry])


def _sb_kernel(qt_ref, kt_ref, vt_ref, o_ref, *, t):
    qi = pl.program_id(1)
    strict = _iota((t, t), 0) < _iota((t, t), 1)
    sub = min(t, 256)
    later = (_iota((sub, sub), 1) > _iota((sub, sub), 0)).astype(BF16)

    def step(kb, carry, masked):
        ks = pl.multiple_of(kb * t, t)
        out = []
        for h in range(FOX_HEADS):
            run, acc = carry[h]
            kvh = h // 2
            z = _tn(kt_ref[HD * kvh:HD * (kvh + 1), pl.ds(ks, t)], qt_ref[HD * h:HD * (h + 1), :])
            sp = _softplus(z)
            l1m = -sp
            if masked:
                l1m = jnp.where(strict, l1m, 0.0)
            revs = []
            for j in reversed(range(t // sub)):
                x = l1m[sub * j:sub * (j + 1)]
                r = sum(_mm(later, p) for p in _split_bf16(x, 2)) + run
                revs.insert(0, r)
                run = r[0:1, :] + x[0:1, :]
            w = jnp.exp((z - sp) + jnp.concatenate(revs, axis=0))
            if masked:
                w = jnp.where(strict, w, 0.0)
            acc = acc + _mm(vt_ref[HD * kvh:HD * (kvh + 1), pl.ds(ks, t)], w.astype(BF16))
            out.append((run, acc))
        return tuple(out)

    def alive(carry):
        top = functools.reduce(jnp.maximum, [c[0] for c in carry])
        return jnp.max(top) > SB_DEAD

    init = tuple((jnp.zeros((1, t), F32), jnp.zeros((HD, t), F32)) for _ in range(FOX_HEADS))
    carry = step(qi, init, True)

    def body(state):
        kb, _, carry = state
        carry = step(kb, carry, False)
        return kb - 1, alive(carry), carry

    _, _, carry = lax.while_loop(lambda st: jnp.logical_and(st[0] >= 0, st[1]), body,
                                 (qi - 1, alive(carry), carry))
    _store_heads(o_ref, [acc for _, acc in carry])


def _diff_lambda(lam_ref, lam_init):
    lf = lam_ref[...]
    a = jnp.sum(lf[0:1] * lf[1:2], axis=-1, keepdims=True)
    b = jnp.sum(lf[2:3] * lf[3:4], axis=-1, keepdims=True)
    return jnp.exp(a) - jnp.exp(b) + lam_init


def _diff_kernel(qt_ref, k_ref, vt_ref, bias_ref, lam_ref, gsub_ref, o_ref, *, t, lam_init):
    qi = pl.program_id(1)
    mask = _causal_mask(t)
    row_hi = _iota((LANES, t), 0) >= HD
    qm = []
    for h in range(DIFF_HEADS):
        q = qt_ref[LANES * h:LANES * (h + 1), :]
        zero = jnp.zeros_like(q)
        qm += [jnp.where(row_hi, zero, q), jnp.where(row_hi, q, zero)]

    def step(kb, carry, masked):
        ks = pl.multiple_of(kb * t, t)
        d = jnp.minimum(qi - kb, 2)
        out = []
        for h in range(DIFF_HEADS):
            lo = LANES * (h // 2)
            k = k_ref[pl.ds(ks, t), lo:lo + LANES]
            vt = vt_ref[lo:lo + LANES, pl.ds(ks, t)]
            bias = bias_ref[d, h]
            for mp in range(2):
                s = _mm(k, qm[2 * h + mp]) + bias
                if masked:
                    s = jnp.where(mask, s, NEG)
                out.append(_softmax_update(s, vt, carry[2 * h + mp]))
        return tuple(out)

    init = tuple(_softmax_init(t, LANES) for _ in range(2 * DIFF_HEADS))
    carry = lax.fori_loop(0, qi, functools.partial(step, masked=False), init)
    carry = step(qi, carry, True)
    lam = _diff_lambda(lam_ref, lam_init)
    for h in range(DIFF_HEADS):
        (_, l0, a0), (_, l1, a1) = carry[2 * h], carry[2 * h + 1]
        o = (a0 / l0 - lam * (a1 / l1)).T
        o_ref[:, LANES * h:LANES * (h + 1)] = _rms(o, gsub_ref[...]) * (1.0 - lam_init)


def _mem_kernel(q_ref, k_ref, v_ref, o_ref):
    for h in range(MEM_HEADS):
        sl = slice(MEM_HD * h, MEM_HD * (h + 1))
        s = _nt(q_ref[:, sl], k_ref[:, sl]) * MEM_HD ** -0.5
        e = jnp.exp(s - jnp.max(s, axis=-1, keepdims=True))
        p = e / jnp.sum(e, axis=-1, keepdims=True)
        o_ref[:, sl] = _mm(p.astype(BF16), v_ref[:, sl])


def _attn_call(body, args, in_specs, B, S, t):
    return pl.pallas_call(
        body, grid=(B, S // t), in_specs=in_specs,
        out_specs=pl.BlockSpec((None, t, BRANCH_W), lambda b, i: (b, i, 0)),
        out_shape=jax.ShapeDtypeStruct((B, S, BRANCH_W), F32),
        compiler_params=_cparams(("parallel", "arbitrary"), VMEM_LIMIT),
    )(*args)


def _q_spec(t, w):
    return pl.BlockSpec((None, t, w), lambda b, i: (b, i, 0))


def _qt_spec(t):
    return pl.BlockSpec((None, 512, t), lambda b, i: (b, 0, i))


def _batch_spec(a):
    return pl.BlockSpec((None,) + a.shape[1:], lambda b, i: (b, 0, 0))


def _kt_specs(S):
    return [pl.BlockSpec((None, 256, S), lambda b, i: (b, 0, 0)),
            pl.BlockSpec((None, 256, S), lambda b, i: (b, 1, 0))]


def _kv_specs(n, w):
    return [pl.BlockSpec((None, n, w), lambda b, i: (b, 0, 0)),
            pl.BlockSpec((None, n, w), lambda b, i: (b, 0, 1))]


def _const_spec(a):
    return pl.BlockSpec(a.shape, lambda b, i: (0,) * a.ndim, pipeline_mode=pl.Buffered(1))


def _merge_kernel(x_ref, gpre_ref, gpost_ref, of_ref, os_ref, od_ref, om_ref,
                  wgpt_ref, wmgt_ref, wbr_ref, wout_ref, y_ref):
    x = x_ref[...]
    hb = _rms(x, gpre_ref[...]).astype(BF16)
    y = jnp.zeros(x.shape, F32)
    for n, o_ref in enumerate((of_ref, os_ref, od_ref, om_ref)):
        gp = _nt(hb, wgpt_ref[BRANCH_W * n:BRANCH_W * (n + 1), :])
        o = o_ref[...] * (gp * _sigmoid(gp))
        u = _mm(o.astype(BF16), wbr_ref[n])
        mg = _nt(hb, wmgt_ref[D_MODEL * n:D_MODEL * (n + 1), :])
        y = y + _sigmoid(mg) * u
    out = _mm(y.astype(BF16), wout_ref[...])
    y_ref[...] = x + _rms(out, gpost_ref[...])


def _merge_call(x2, gpre, gpost, outs, wgpt, wmgt, wbr, wout, tm):
    T = x2.shape[0]
    row = lambda n: pl.BlockSpec((tm, n), lambda i: (i, 0))

    def const(a):
        return pl.BlockSpec(a.shape, lambda i: (0,) * a.ndim, pipeline_mode=pl.Buffered(1))

    return pl.pallas_call(
        _merge_kernel, grid=(T // tm,),
        in_specs=[row(D_MODEL), const(gpre), const(gpost)] + [row(BRANCH_W)] * 4
        + [const(wgpt), const(wmgt), const(wbr), const(wout)],
        out_specs=row(D_MODEL),
        out_shape=jax.ShapeDtypeStruct((T, D_MODEL), F32),
        compiler_params=_cparams(("parallel",), VMEM_LIMIT),
    )(x2, gpre, gpost, *outs, wgpt, wmgt, wbr, wout)


def _suffix_terms(x, run, pps):
    R = 8 * pps
    tri = (_iota((LANES, LANES), 0) > _iota((LANES, LANES), 1)).astype(BF16)
    ones = jnp.ones((LANES, LANES), BF16)
    r0, r1 = _iota((R, R), 0), _iota((R, R), 1)
    newer = ((r1 % 8 == r0 % 8) & (r1 < r0)).astype(BF16)
    parts = _split_bf16(x, 3)
    within = sum(_mm(p, tri) for p in parts)
    total = sum(_mm(p, ones) for p in parts)
    cross = sum(_mm(newer, p) for p in _split_bf16(total, 3))
    run_rows = jnp.concatenate([run] * pps, axis=0)
    last = slice(R - 8, R)
    new_run = run + cross[last, 0:1] + total[last, 0:1]
    return within + cross + run_rows, new_run


def _group_max(x, pps):
    return functools.reduce(jnp.maximum, [x[8 * i:8 * (i + 1)] for i in range(pps)])


def _group_sum(x, pps):
    return functools.reduce(jnp.add, [x[8 * i:8 * (i + 1)] for i in range(pps)])


def _decode_kernel(pt_ref, *refs, kind, pps, lam_init):
    del pt_ref
    it = iter(refs)
    q_ref, kvnew_ref = next(it), next(it)
    pages = [next(it) for _ in range(pps)]
    if kind == "fox":
        lfnew_ref = next(it)
        lfs = [next(it) for _ in range(pps)]
    if kind == "diff":
        dbias_ref, lam_ref, gsub_ref = next(it), next(it), next(it)
    o_ref = next(it)
    m_ref, l_ref, acc_ref, run_ref = next(it), next(it), next(it), next(it)
    g = pl.program_id(1)
    q = q_ref[...]

    @pl.when(g == 0)
    def _():
        if kind == "sb":
            run_ref[...] = jnp.zeros_like(run_ref)
            acc_ref[...] = jnp.zeros_like(acc_ref)
        else:
            knew = kvnew_ref[:, 0:256].astype(BF16).astype(F32)
            vnew = kvnew_ref[:, 256:512].astype(BF16).astype(F32)
            s = jnp.sum(q.astype(F32) * knew, axis=-1, keepdims=True)
            if kind == "diff":
                s = s + dbias_ref[2][:, 0:1]
            m_ref[...] = s
            l_ref[...] = jnp.ones_like(l_ref)
            acc_ref[...] = jnp.broadcast_to(vnew, acc_ref.shape)
            if kind == "fox":
                run_ref[...] = lfnew_ref[...]

    if kind == "diff":
        def keys(page, c):
            return page[pl.ds(c, PAGE, stride=4), :].astype(BF16)
        s = jnp.concatenate(
            [_nt(q[:, 0:LANES], keys(p, 0)) + _nt(q[:, LANES:2 * LANES], keys(p, 1)) for p in pages],
            axis=0)

        def pv(w, page):
            return jnp.concatenate([_mm(w, keys(page, 2)), _mm(w, keys(page, 3))], axis=1)
    else:
        s = jnp.concatenate([_mm(q, p[0:256, :].astype(BF16)) for p in pages], axis=0)

        def pv(w, page):
            return _nt(w, page[256:512, :].astype(BF16))

    def weighted_values(w):
        wb = w.astype(BF16)
        return sum(pv(wb[8 * i:8 * (i + 1)], pages[i]) for i in range(pps))

    if kind == "sb":
        sp = _softplus(s)
        rev, new_run = _suffix_terms(-sp, run_ref[...], pps)
        acc_ref[...] = acc_ref[...] + weighted_values(jnp.exp((s - sp) + rev))
        run_ref[...] = new_run
    else:
        if kind == "fox":
            decay, new_run = _suffix_terms(
                jnp.concatenate([lf[...] for lf in lfs], axis=0), run_ref[...], pps)
            s = s + decay
            run_ref[...] = new_run
        else:
            first = jnp.where(g == 0, dbias_ref[0], dbias_ref[1])
            s = s + jnp.concatenate([first] + [dbias_ref[1]] * (pps - 1), axis=0)
        m = m_ref[...]
        m_new = jnp.maximum(m, _group_max(jnp.max(s, axis=-1, keepdims=True), pps))
        alpha = jnp.exp(m - m_new)
        p = jnp.exp(s - jnp.concatenate([m_new] * pps, axis=0))
        l_ref[...] = alpha * l_ref[...] + _group_sum(jnp.sum(p, axis=-1, keepdims=True), pps)
        acc_ref[...] = alpha * acc_ref[...] + weighted_values(p)
        m_ref[...] = m_new

    @pl.when(g == pl.num_programs(1) - 1)
    def _():
        if kind == "diff":
            n = acc_ref[...] / l_ref[...]
            lam = _diff_lambda(lam_ref, lam_init)
            o = n[0:DIFF_HEADS] - lam * n[DIFF_HEADS:2 * DIFF_HEADS]
            row = _iota((DIFF_HEADS, LANES), 0)
            o = jnp.where(row // 2 == 0, o[:, 0:LANES], o[:, LANES:2 * LANES])
            o_ref[...] = _rms(o, gsub_ref[...]) * (1.0 - lam_init)
        else:
            o = acc_ref[...] if kind == "sb" else acc_ref[...] / l_ref[...]
            row = _iota((FOX_HEADS, HD), 0)
            res = o[:, 0:HD]
            for kvh in range(1, 4):
                res = jnp.where(row // 2 == kvh, o[:, HD * kvh:HD * (kvh + 1)], res)
            o_ref[...] = res


def _decode_call(kind, layer, pt, qbd, kvnew, pool, extra, lam_init, pps):
    B, n_pages = pt.shape
    steps = n_pages // pps
    oshape = (DIFF_HEADS, LANES) if kind == "diff" else (FOX_HEADS, HD)

    def page_map(i):
        return lambda b, g, pt: (layer, pt[b, n_pages - 1 - (g * pps + i)], 0, 0)

    per_b = lambda a: pl.BlockSpec((None,) + a.shape[1:], lambda b, g, pt: (b,) + (0,) * (a.ndim - 1))
    const = lambda a: pl.BlockSpec(a.shape, lambda b, g, pt: (0,) * a.ndim)
    in_specs = [per_b(qbd), per_b(kvnew)]
    in_specs += [pl.BlockSpec((None, None, 512, LANES), page_map(i)) for i in range(pps)]
    args = [qbd, kvnew] + [pool] * pps
    if kind == "fox":
        lfnew, lfpool = extra
        in_specs += [per_b(lfnew)]
        in_specs += [pl.BlockSpec((None, None, FOX_HEADS, PAGE), page_map(i)) for i in range(pps)]
        args += [lfnew] + [lfpool] * pps
    elif kind == "diff":
        in_specs += [const(a) for a in extra]
        args += list(extra)
    grid_spec = pltpu.PrefetchScalarGridSpec(
        num_scalar_prefetch=1, grid=(B, steps), in_specs=in_specs,
        out_specs=pl.BlockSpec((None,) + oshape, lambda b, g, pt: (b, 0, 0)),
        scratch_shapes=[pltpu.VMEM((8, 1), F32), pltpu.VMEM((8, 1), F32),
                        pltpu.VMEM((8, 256), F32), pltpu.VMEM((8, 1), F32)])
    return pl.pallas_call(
        functools.partial(_decode_kernel, kind=kind, pps=pps, lam_init=lam_init),
        grid_spec=grid_spec,
        out_shape=jax.ShapeDtypeStruct((B,) + oshape, F32),
        compiler_params=_cparams(("parallel", "arbitrary"), VMEM_LIMIT),
    )(pt, *args)


def _memdec_kernel(q_ref, kv_ref, o_ref):
    n = kv_ref.shape[0] // (2 * MEM_HEADS)

    def rows(c):
        return kv_ref[pl.ds(c, n, stride=2 * MEM_HEADS), :].astype(BF16)

    q = q_ref[...]
    s = sum(_nt(q[:, MEM_HD * h:MEM_HD * (h + 1)], rows(h)) for h in range(MEM_HEADS))
    s = s * MEM_HD ** -0.5
    e = jnp.exp(s - jnp.max(s, axis=-1, keepdims=True))
    p = (e / jnp.sum(e, axis=-1, keepdims=True)).astype(BF16)
    row = _iota((8, MEM_HD), 0)
    res = jnp.zeros((8, MEM_HD), F32)
    for h in range(MEM_HEADS):
        res = jnp.where(row == h, _mm(p, rows(MEM_HEADS + h)), res)
    o_ref[...] = res[0:MEM_HEADS]


def _memdec_call(layer, qbd, cache):
    B = qbd.shape[0]
    return pl.pallas_call(
        _memdec_kernel, grid=(B,),
        in_specs=[pl.BlockSpec((None, 8, 512), lambda b: (b, 0, 0)),
                  pl.BlockSpec((None, None) + cache.shape[2:], lambda b: (layer, b, 0, 0))],
        out_specs=pl.BlockSpec((None, MEM_HEADS, MEM_HD), lambda b: (b, 0, 0)),
        out_shape=jax.ShapeDtypeStruct((B, MEM_HEADS, MEM_HD), F32),
        compiler_params=_cparams(("parallel",), VMEM_LIMIT),
    )(qbd, cache)


def _pick(n, pref):
    t = min(n, pref)
    while n % t:
        t //= 2
    return t


def _layer_weights(w_in_l, b_f_l, w_branch_l, w_out_l):
    c = (0, 512, 768, 1024, 1032, 1544, 1800, 2056, 2568, 2824, 3080, 3592, 5640, 9736)
    wt = jnp.transpose(w_in_l)
    return dict(
        wt=jnp.concatenate([wt[c[0]:c[3]], wt[c[4]:c[11]]], axis=0).astype(BF16),
        wft=wt[c[3]:c[4]].astype(BF16), bft=b_f_l.reshape(FOX_HEADS, 1),
        wgpt=wt[c[11]:c[12]].astype(BF16), wmgt=wt[c[12]:c[13]].astype(BF16),
        wbr=w_branch_l.astype(BF16), wout=w_out_l.astype(BF16))


def kernel(x_prompt, x_sample, cache_fox_kv, cache_fox_logf, cache_sb_kv, cache_diff_kv, cache_mem_kv,
           page_table, mem_prompt, rel_bias, g_pre, g_post, g_mem, w_in, b_f, w_mem_kv, lam,
           g_diff_sub, w_branch, w_out):
    B, S, D = x_prompt.shape
    DB = x_sample.shape[0]
    depth = w_in.shape[0]
    n_pool = cache_fox_kv.shape[1]
    n_pages = page_table.shape[1]
    mem_len = mem_prompt.shape[1]
    T = B * S
    t_att = _pick(S, 512)
    t_fox = _pick(S, 512)
    t_sb = _pick(S, 512)
    tm_proj = _pick(S, 512)
    tm_merge = _pick(T, 512)
    pps = _pick(n_pages, 64)

    feat_major = lambda c: jnp.transpose(c, (0, 1, 3, 4, 5, 2)).reshape(depth, n_pool, 512, PAGE)
    fox_pool = feat_major(cache_fox_kv)
    sb_pool = feat_major(cache_sb_kv)
    diff_pool = cache_diff_kv.reshape(depth, n_pool, 4 * PAGE, LANES)
    lf_pool = jnp.swapaxes(cache_fox_logf, 2, 3)
    mem_cache = cache_mem_kv.reshape(depth, DB, 2 * MEM_HEADS * mem_len, MEM_HD)
    bias_tiles, dec_bias = _bias_call(rel_bias, t_att)

    h8 = jnp.arange(8)
    sel_kv = (jnp.arange(4)[None, :] == (h8 // 2)[:, None]).astype(BF16)
    r8m, r8h = h8 // 4, h8 % 4
    sel_diff = ((jnp.arange(2)[None, :, None] == (r8h // 2)[:, None, None])
                & (jnp.arange(2)[None, None, :] == r8m[:, None, None])).astype(BF16)
    sel_mem = (jnp.arange(8)[:, None] == jnp.arange(4)[None, :]).astype(BF16)

    xp = x_prompt.reshape(T, D)
    xs = x_sample.reshape(DB, D)
    mem2 = mem_prompt.reshape(B * mem_len, D)
    st_p = [[] for _ in range(5)]
    st_s = [[] for _ in range(4)]
    for l in range(depth):
        lam_init = 0.8 - 0.6 * math.exp(-0.3 * l)
        w = _layer_weights(w_in[l], b_f[l], w_branch[l], w_out[l])
        gpre, gpost = g_pre[l].reshape(1, D), g_post[l].reshape(1, D)
        gsub = g_diff_sub[l].reshape(1, 2 * HD)

        (qft, ktfb, ktf, lft, qst, ktsb, kts, qdt, kdb, vdtb, kvdi, qm) = _proj_call(
            xp, gpre, w["wt"], w["wft"], w["bft"], B, S, tm_proj)
        mkvi, mkvb = _memproj_call(mem2, g_mem[l].reshape(1, D), w_mem_kv[l].astype(BF16),
                                   _pick(B * mem_len, 512))
        f_row, f_aug = _cumsum_call(lft)
        r3 = lambda a: a.reshape(B, S, a.shape[-1])
        o_fox = _attn_call(
            functools.partial(_fox_kernel, t=t_fox), [qft, ktfb, ktfb, f_aug, f_row],
            [_qt_spec(t_fox)] + _kt_specs(S) + [_batch_spec(f_aug), _batch_spec(f_row)], B, S, t_fox)
        o_sb = _attn_call(
            functools.partial(_sb_kernel, t=t_sb), [qst, ktsb, ktsb],
            [_qt_spec(t_sb)] + _kt_specs(S), B, S, t_sb)
        kd3 = r3(kdb)
        o_diff = _attn_call(
            functools.partial(_diff_kernel, t=t_att, lam_init=lam_init),
            [qdt, kd3, vdtb, bias_tiles, lam[l], gsub],
            [_qt_spec(t_att), _batch_spec(kd3), _batch_spec(vdtb),
             _const_spec(bias_tiles), _const_spec(lam[l]), _const_spec(gsub)], B, S, t_att)
        mkv3 = mkvb.reshape(B, mem_len, 1024)
        o_mem = _attn_call(_mem_kernel, [r3(qm), mkv3, mkv3],
                           [_q_spec(t_att, 512)] + _kv_specs(mem_len, 512), B, S, t_att)
        flat = lambda a: a.reshape(T, BRANCH_W)
        xp = _merge_call(xp, gpre, gpost, [flat(o_fox), flat(o_sb), flat(o_diff), flat(o_mem)],
                         w["wgpt"], w["wmgt"], w["wbr"], w["wout"], tm_merge)
        tok_major = lambda kt: jnp.transpose(kt.reshape(B, 2, 4, HD, S), (0, 4, 1, 2, 3))
        st_p[0].append(tok_major(ktf))
        st_p[1].append(jnp.swapaxes(lft, 1, 2))
        st_p[2].append(tok_major(kts))
        st_p[3].append(kvdi.reshape(B, S, 2, 2, 2 * HD))
        st_p[4].append(mkvi.reshape(B, mem_len, 2, MEM_HEADS, MEM_HD))

        (qft, _, ktf, lft, qst, _, kts, qdt, _, _, kvdi, qm) = _proj_call(
            xs, gpre, w["wt"], w["wft"], w["bft"], 1, DB, DB)
        kvf, kvs = jnp.transpose(ktf[0]), jnp.transpose(kts[0])
        kvd = kvdi.reshape(DB, 512)
        lfs = jnp.transpose(lft[0])
        qf, qs, qd = jnp.transpose(qft[0]), jnp.transpose(qst[0]), jnp.transpose(qdt[0])
        place = lambda q: (q.reshape(DB, 8, 1, HD) * sel_kv[None, :, :, None]).reshape(DB, 8, 256)
        qd_rows = jnp.tile(qd.reshape(DB, 1, DIFF_HEADS, 1, 2, HD), (1, 2, 1, 1, 1, 1)).reshape(DB, 8, 1, 2, HD)
        qbd_diff = (qd_rows * sel_diff[None, :, :, :, None]).reshape(DB, 8, 256)
        qbd_mem = (qm.reshape(DB, 1, MEM_HEADS, MEM_HD)
                   * sel_mem[None, :, :, None]).reshape(DB, 8, 512)
        new = lambda kv: kv.reshape(DB, 1, 512)
        a_fox = _decode_call("fox", l, page_table, place(qf), new(kvf), fox_pool,
                             (lfs.reshape(DB, FOX_HEADS, 1), lf_pool), lam_init, pps)
        a_sb = _decode_call("sb", l, page_table, place(qs), new(kvs), sb_pool, (), lam_init, pps)
        a_diff = _decode_call("diff", l, page_table, qbd_diff, new(kvd), diff_pool,
                              (dec_bias, lam[l], gsub), lam_init, pps)
        a_mem = _memdec_call(l, qbd_mem, mem_cache)
        flat_s = lambda a: a.reshape(DB, BRANCH_W)
        xs = _merge_call(xs, gpre, gpost, [flat_s(a_fox), flat_s(a_sb), flat_s(a_diff), flat_s(a_mem)],
                         w["wgpt"], w["wmgt"], w["wbr"], w["wout"], DB)
        st_s[0].append(kvf.reshape(DB, 1, 2, 4, HD))
        st_s[1].append(lfs.reshape(DB, 1, FOX_HEADS))
        st_s[2].append(kvs.reshape(DB, 1, 2, 4, HD))
        st_s[3].append(kvd.reshape(DB, 1, 2, 2, 2 * HD))

    return (xp.reshape(B, S, D), xs.reshape(DB, 1, D),
            *[jnp.stack(s, 0) for s in st_p], *[jnp.stack(s, 0) for s in st_s])
```

```python
import functools
import math

import jax
import jax.numpy as jnp
from jax import lax
from jax.experimental import pallas as pl
from jax.experimental.pallas import tpu as pltpu

F32 = jnp.float32
BF16 = jnp.bfloat16

D_MODEL = 1024
N_BRANCH = 4
BRANCH_W = D_MODEL // 2
FOX_HEADS = 8
HD = 64
DIFF_HEADS = 4
MEM_HEADS = 4
MEM_HD = 128
N_BUCKETS = 32
MAX_DISTANCE = 128
EPS = 1e-6
NEG = -1e30
PAGE = 128
LANES = 128
VMEM_LIMIT = 56 * 1024 * 1024
SB_DEAD = -104.0

_R_FQ, _R_FKV, _R_SQ, _R_SKV, _R_DQ, _R_DKV, _R_MQ = 0, 512, 1024, 1536, 2048, 2560, 3072


def _nt(a, b):
    return lax.dot_general(a, b, (((1,), (1,)), ((), ())), preferred_element_type=F32)


def _tn(a, b):
    return lax.dot_general(a, b, (((0,), (0,)), ((), ())), preferred_element_type=F32)


def _mm(a, b):
    return jnp.dot(a, b, preferred_element_type=F32)


def _split_bf16(x, parts):
    out = []
    r = x
    for i in range(parts):
        p = r.astype(BF16)
        out.append(p)
        if i + 1 < parts:
            r = r - p.astype(F32)
    return out


def _mm_split(x, w, parts=3):
    return sum(_mm(p, w) for p in _split_bf16(x, parts))


def _log_sigmoid(x):
    return jnp.minimum(x, 0.0) - jnp.log1p(jnp.exp(-jnp.abs(x)))


def _softplus(x):
    return jnp.maximum(x, 0.0) + jnp.log(1.0 + jnp.exp(-jnp.abs(x)))


def _sigmoid(x):
    return 1.0 / (1.0 + jnp.exp(-x))


def _rms(x, g):
    return x * lax.rsqrt(jnp.mean(x * x, axis=-1, keepdims=True) + EPS) * g


def _cparams(sem, vmem=None):
    return pltpu.CompilerParams(dimension_semantics=sem, vmem_limit_bytes=vmem)


def _iota(shape, dim):
    return lax.broadcasted_iota(jnp.int32, shape, dim)


def _proj_kernel(x_ref, g_ref, wt_ref, wft_ref, bft_ref,
                 qft_ref, ktfb_ref, ktf_ref, lft_ref, qst_ref, ktsb_ref, kts_ref,
                 qdt_ref, kdb_ref, vdtb_ref, kvdi_ref, qm_ref):
    hb = _rms(x_ref[...], g_ref[...]).astype(BF16)
    tm = hb.shape[0]

    def rows(r0, n=512):
        return wt_ref[r0:r0 + n, :]

    qft_ref[...] = (_nt(rows(_R_FQ), hb) * 0.125).astype(BF16)
    kt = _nt(rows(_R_FKV), hb)
    ktf_ref[...] = kt
    ktfb_ref[...] = kt.astype(BF16)
    qst_ref[...] = (_nt(rows(_R_SQ), hb) * 0.125).astype(BF16)
    kt = _nt(rows(_R_SKV), hb)
    kts_ref[...] = kt
    ktsb_ref[...] = kt.astype(BF16)
    qdt_ref[...] = (_nt(rows(_R_DQ), hb) * 0.125).astype(BF16)
    vdtb_ref[...] = _nt(rows(_R_DKV + 256, 256), hb).astype(BF16)
    kv = _nt(hb, rows(_R_DKV))
    kdb_ref[...] = kv[:, 0:256].astype(BF16)
    for c in range(4):
        kvdi_ref[pl.ds(c, tm, stride=4), :] = kv[:, LANES * c:LANES * (c + 1)]
    qm_ref[...] = _nt(hb, rows(_R_MQ)).astype(BF16)
    lft_ref[...] = _log_sigmoid(_nt(wft_ref[...], hb) + bft_ref[...])


def _proj_call(x2, g, wt, wft, bft, B, S, tm):
    T = B * S
    nt = S // tm
    row = lambda n: pl.BlockSpec((tm, n), lambda b, i: (b * nt + i, 0))
    col = lambda n: pl.BlockSpec((None, n, tm), lambda b, i: (b, 0, i))
    full = lambda a: pl.BlockSpec(a.shape, lambda b, i: (0,) * a.ndim)
    sds = jax.ShapeDtypeStruct
    out_shape = (
        sds((B, 512, S), BF16), sds((B, 512, S), BF16), sds((B, 512, S), F32),
        sds((B, FOX_HEADS, S), F32),
        sds((B, 512, S), BF16), sds((B, 512, S), BF16), sds((B, 512, S), F32),
        sds((B, 512, S), BF16), sds((T, 256), BF16), sds((B, 256, S), BF16),
        sds((4 * T, LANES), F32), sds((T, 512), BF16))
    out_specs = (
        col(512), col(512), col(512), col(FOX_HEADS),
        col(512), col(512), col(512),
        col(512), row(256), col(256),
        pl.BlockSpec((4 * tm, LANES), lambda b, i: (b * nt + i, 0)), row(512))
    return pl.pallas_call(
        _proj_kernel, grid=(B, nt),
        in_specs=[row(D_MODEL), full(g), full(wt), full(wft), full(bft)],
        out_specs=out_specs, out_shape=out_shape,
        compiler_params=_cparams(("parallel", "parallel"), VMEM_LIMIT),
    )(x2, g, wt, wft, bft)


def _memproj_kernel(x_ref, g_ref, w_ref, kvi_ref, kvb_ref):
    hb = _rms(x_ref[...], g_ref[...]).astype(BF16)
    tm = hb.shape[0]
    kv = _mm(hb, w_ref[...])
    kvb_ref[...] = kv.astype(BF16)
    for c in range(2 * MEM_HEADS):
        kvi_ref[pl.ds(c, tm, stride=2 * MEM_HEADS), :] = kv[:, MEM_HD * c:MEM_HD * (c + 1)]


def _memproj_call(x2, g, w, tm):
    T, N = x2.shape[0], w.shape[1]
    return pl.pallas_call(
        _memproj_kernel, grid=(T // tm,),
        in_specs=[pl.BlockSpec((tm, D_MODEL), lambda i: (i, 0)),
                  pl.BlockSpec(g.shape, lambda i: (0, 0)),
                  pl.BlockSpec(w.shape, lambda i: (0, 0))],
        out_specs=(pl.BlockSpec((2 * MEM_HEADS * tm, MEM_HD), lambda i: (i, 0)),
                   pl.BlockSpec((tm, N), lambda i: (i, 0))),
        out_shape=(jax.ShapeDtypeStruct((2 * MEM_HEADS * T, MEM_HD), F32),
                   jax.ShapeDtypeStruct((T, N), BF16)),
        compiler_params=_cparams(("parallel",), VMEM_LIMIT),
    )(x2, g, w)


AUG = 16


def _aug_rows(f2, width, lead):
    row = _iota((AUG, width), 0)
    out = jnp.zeros((AUG, width), F32)
    if lead is None:
        out = jnp.where((row >= 6) & (row < 9), 1.0, out)
        srcs = [(3 * g + i, p) for g in range(2) for i, p in enumerate(_split_bf16(f2[g:g + 1], 3))]
    else:
        out = jnp.where(row // 3 == lead, -1.0, out)
        srcs = [(6 + i, p) for i, p in enumerate(_split_bf16(f2[lead:lead + 1], 3))]
    for r, p in srcs:
        out = jnp.where(row == r, jnp.broadcast_to(p.astype(F32), (AUG, width)), out)
    return out.astype(BF16)


def _cumsum_kernel(lft_ref, f_ref, fa_ref, *, chunk):
    S = lft_ref.shape[1]
    tri = (_iota((chunk, chunk), 0) <= _iota((chunk, chunk), 1)).astype(BF16)
    carry = jnp.zeros((FOX_HEADS, 1), F32)
    for c in range(S // chunk):
        sl = slice(c * chunk, (c + 1) * chunk)
        loc = _mm_split(lft_ref[:, sl], tri)
        f = loc + carry
        f_ref[:, sl] = f
        for kvh in range(FOX_HEADS // 2):
            fa_ref[AUG * kvh:AUG * (kvh + 1), sl] = _aug_rows(f[2 * kvh:2 * kvh + 2], chunk, None)
        carry = carry + loc[:, chunk - 1:chunk]


def _cumsum_call(lft):
    B, _, S = lft.shape
    spec = pl.BlockSpec((None, FOX_HEADS, S), lambda b: (b, 0, 0))
    na = AUG * FOX_HEADS // 2
    return pl.pallas_call(
        functools.partial(_cumsum_kernel, chunk=min(256, S)), grid=(B,),
        in_specs=[spec], out_specs=(spec, pl.BlockSpec((None, na, S), lambda b: (b, 0, 0))),
        out_shape=(jax.ShapeDtypeStruct(lft.shape, F32), jax.ShapeDtypeStruct((B, na, S), BF16)),
        compiler_params=_cparams(("parallel",)),
    )(lft)


def _t5_bucket(dist):
    max_exact = N_BUCKETS // 2
    d1 = jnp.maximum(dist, 1).astype(F32)
    large = max_exact + (jnp.log(d1 / max_exact) / math.log(MAX_DISTANCE / max_exact)
                         * (N_BUCKETS - max_exact)).astype(jnp.int32)
    large = jnp.minimum(large, N_BUCKETS - 1)
    return jnp.where(dist < max_exact, dist, large)


def _bias_kernel(rb_ref, tiles_ref, dec_ref, *, tile):
    key = _iota((tile, tile), 0)
    qry = _iota((tile, tile), 1)
    for d in range(3):
        bkt = _t5_bucket(jnp.maximum(tile * d + qry - key, 0))
        for h in range(DIFF_HEADS):
            acc = jnp.zeros((tile, tile), F32)
            for b in range(N_BUCKETS):
                acc = jnp.where(bkt == b, rb_ref[b, h], acc)
            tiles_ref[d, h] = acc
    r = _iota((2 * DIFF_HEADS, LANES), 0) % DIFF_HEADS
    lane = _iota((2 * DIFF_HEADS, LANES), 1)
    for idx, dist in enumerate((PAGE - lane, jnp.full_like(lane, 2 * MAX_DISTANCE),
                                jnp.zeros_like(lane))):
        bkt = _t5_bucket(dist)
        acc = jnp.zeros((2 * DIFF_HEADS, LANES), F32)
        for h in range(DIFF_HEADS):
            for b in range(N_BUCKETS):
                acc = jnp.where((bkt == b) & (r == h), rb_ref[b, h], acc)
        dec_ref[idx] = acc


def _bias_call(rel_bias, tile):
    return pl.pallas_call(
        functools.partial(_bias_kernel, tile=tile),
        in_specs=[pl.BlockSpec(memory_space=pltpu.SMEM)],
        out_specs=(pl.BlockSpec(memory_space=pltpu.VMEM),) * 2,
        out_shape=(jax.ShapeDtypeStruct((3, DIFF_HEADS, tile, tile), F32),
                   jax.ShapeDtypeStruct((3, 2 * DIFF_HEADS, LANES), F32)),
    )(rel_bias)


def _causal_mask(t):
    return _iota((t, t), 0) <= _iota((t, t), 1)


def _softmax_update(s, vt, carry):
    m, l, acc = carry
    m_new = jnp.maximum(m, jnp.max(s, axis=0, keepdims=True))
    alpha = jnp.exp(m - m_new)
    p = jnp.exp(s - m_new)
    l = alpha * l + jnp.sum(p, axis=0, keepdims=True)
    acc = alpha * acc + _mm(vt, p.astype(BF16))
    return m_new, l, acc


def _softmax_init(t, rows):
    return (jnp.full((1, t), NEG, F32), jnp.zeros((1, t), F32), jnp.zeros((rows, t), F32))


def _store_heads(o_ref, outs):
    for j in range(len(outs) // 2):
        pair = jnp.concatenate([outs[2 * j], outs[2 * j + 1]], axis=0)
        o_ref[:, LANES * j:LANES * (j + 1)] = pair.T


def _fox_kernel(qt_ref, kt_ref, vt_ref, fa_ref, frow_ref, o_ref, *, t):
    qi = pl.program_id(1)
    q0 = pl.multiple_of(qi * t, t)
    mask = _causal_mask(t)
    qa = [jnp.concatenate(
        [qt_ref[HD * h:HD * (h + 1), :],
         _aug_rows(frow_ref[2 * (h // 2):2 * (h // 2) + 2, pl.ds(q0, t)], t, h % 2)], axis=0)
        for h in range(FOX_HEADS)]

    def step(kb, carry, masked):
        ks = pl.multiple_of(kb * t, t)
        out = []
        for h in range(FOX_HEADS):
            kvh = h // 2
            ka = jnp.concatenate([kt_ref[HD * kvh:HD * (kvh + 1), pl.ds(ks, t)],
                                  fa_ref[AUG * kvh:AUG * (kvh + 1), pl.ds(ks, t)]], axis=0)
            s = _tn(ka, qa[h])
            if masked:
                s = jnp.where(mask, s, NEG)
            out.append(_softmax_update(s, vt_ref[HD * kvh:HD * (kvh + 1), pl.ds(ks, t)], carry[h]))
        return tuple(out)

    init = tuple(_softmax_init(t, HD) for _ in range(FOX_HEADS))
    carry = lax.fori_loop(0, qi, functools.partial(step, masked=False), init)
    carry = step(qi, carry, True)
    _store_heads(o_ref, [acc / l for _, l, acc in carry])


def _sb_kernel(qt_ref, kt_ref, vt_ref, o_ref, *, t):
    qi = pl.program_id(1)
    strict = _iota((t, t), 0) < _iota((t, t), 1)
    sub = min(t // 2, 256)
    later = (_iota((sub, sub), 1) > _iota((sub, sub), 0)).astype(BF16)

    def step(kb, carry, masked, nq=t, nk=t):
        ks = pl.multiple_of(kb * nk, nk)
        out = []
        for h in range(FOX_HEADS):
            run, acc = carry[h]
            kvh = h // 2
            run_c = run[:, :nq]
            z = _tn(kt_ref[HD * kvh:HD * (kvh + 1), pl.ds(ks, nk)], qt_ref[HD * h:HD * (h + 1), 0:nq])
            sp = _softplus(z)
            l1m = -sp
            if masked:
                l1m = jnp.where(strict, l1m, 0.0)
            revs = []
            for j in reversed(range(nk // sub)):
                x = l1m[sub * j:sub * (j + 1)]
                r = sum(_mm(later, p) for p in _split_bf16(x, 2)) + run_c
                revs.insert(0, r)
                run_c = r[0:1, :] + x[0:1, :]
            w = jnp.exp((z - sp) + jnp.concatenate(revs, axis=0))
            if masked:
                w = jnp.where(strict, w, 0.0)
            acc_c = acc[:, :nq] + _mm(vt_ref[HD * kvh:HD * (kvh + 1), pl.ds(ks, nk)], w.astype(BF16))
            if nq < t:
                run_c = jnp.concatenate([run_c, run[:, nq:]], axis=1)
                acc_c = jnp.concatenate([acc_c, acc[:, nq:]], axis=1)
            out.append((run_c, acc_c))
        return tuple(out)

    def alive(carry, lo=0):
        top = functools.reduce(jnp.maximum, [c[0][:, lo:] for c in carry])
        return jnp.max(top) > SB_DEAD

    init = tuple((jnp.zeros((1, t), F32), jnp.zeros((HD, t), F32)) for _ in range(FOX_HEADS))
    carry = step(qi, init, True)

    def body(state):
        kb, _, carry = state
        carry = lax.cond(alive(carry, t // 2), lambda c: step(kb, c, False, t, t // 2),
                         lambda c: step(kb, c, False, t // 2, t // 2), carry)
        return kb - 1, alive(carry), carry

    _, _, carry = lax.while_loop(lambda st: jnp.logical_and(st[0] >= 0, st[1]), body,
                                 (2 * qi - 1, alive(carry), carry))
    _store_heads(o_ref, [acc for _, acc in carry])


def _diff_lambda(lam_ref, lam_init):
    lf = lam_ref[...]
    a = jnp.sum(lf[0:1] * lf[1:2], axis=-1, keepdims=True)
    b = jnp.sum(lf[2:3] * lf[3:4], axis=-1, keepdims=True)
    return jnp.exp(a) - jnp.exp(b) + lam_init


def _diff_kernel(qt_ref, k_ref, vt_ref, bias_ref, lam_ref, gsub_ref, o_ref, *, t, lam_init):
    qi = pl.program_id(1)
    mask = _causal_mask(t)
    row_hi = _iota((LANES, t), 0) >= HD
    qm = []
    for h in range(DIFF_HEADS):
        q = qt_ref[LANES * h:LANES * (h + 1), :]
        zero = jnp.zeros_like(q)
        qm += [jnp.where(row_hi, zero, q), jnp.where(row_hi, q, zero)]

    def step(kb, carry, masked):
        ks = pl.multiple_of(kb * t, t)
        d = jnp.minimum(qi - kb, 2)
        out = []
        for h in range(DIFF_HEADS):
            lo = LANES * (h // 2)
            k = k_ref[pl.ds(ks, t), lo:lo + LANES]
            vt = vt_ref[lo:lo + LANES, pl.ds(ks, t)]
            bias = bias_ref[d, h]
            for mp in range(2):
                s = _mm(k, qm[2 * h + mp]) + bias
                if masked:
                    s = jnp.where(mask, s, NEG)
                out.append(_softmax_update(s, vt, carry[2 * h + mp]))
        return tuple(out)

    init = tuple(_softmax_init(t, LANES) for _ in range(2 * DIFF_HEADS))
    carry = lax.fori_loop(0, qi, functools.partial(step, masked=False), init)
    carry = step(qi, carry, True)
    lam = _diff_lambda(lam_ref, lam_init)
    for h in range(DIFF_HEADS):
        (_, l0, a0), (_, l1, a1) = carry[2 * h], carry[2 * h + 1]
        o = (a0 / l0 - lam * (a1 / l1)).T
        o_ref[:, LANES * h:LANES * (h + 1)] = _rms(o, gsub_ref[...]) * (1.0 - lam_init)


def _mem_kernel(q_ref, k_ref, v_ref, o_ref):
    for h in range(MEM_HEADS):
        sl = slice(MEM_HD * h, MEM_HD * (h + 1))
        s = _nt(q_ref[:, sl], k_ref[:, sl]) * MEM_HD ** -0.5
        e = jnp.exp(s - jnp.max(s, axis=-1, keepdims=True))
        p = e / jnp.sum(e, axis=-1, keepdims=True)
        o_ref[:, sl] = _mm(p.astype(BF16), v_ref[:, sl])


def _attn_call(body, args, in_specs, B, S, t):
    return pl.pallas_call(
        body, grid=(B, S // t), in_specs=in_specs,
        out_specs=pl.BlockSpec((None, t, BRANCH_W), lambda b, i: (b, i, 0)),
        out_shape=jax.ShapeDtypeStruct((B, S, BRANCH_W), F32),
        compiler_params=_cparams(("parallel", "arbitrary"), VMEM_LIMIT),
    )(*args)


def _q_spec(t, w):
    return pl.BlockSpec((None, t, w), lambda b, i: (b, i, 0))


def _qt_spec(t):
    return pl.BlockSpec((None, 512, t), lambda b, i: (b, 0, i))


def _batch_spec(a):
    return pl.BlockSpec((None,) + a.shape[1:], lambda b, i: (b, 0, 0))


def _kt_specs(S):
    return [pl.BlockSpec((None, 256, S), lambda b, i: (b, 0, 0)),
            pl.BlockSpec((None, 256, S), lambda b, i: (b, 1, 0))]


def _kv_specs(n, w):
    return [pl.BlockSpec((None, n, w), lambda b, i: (b, 0, 0)),
            pl.BlockSpec((None, n, w), lambda b, i: (b, 0, 1))]


def _const_spec(a):
    return pl.BlockSpec(a.shape, lambda b, i: (0,) * a.ndim, pipeline_mode=pl.Buffered(1))


def _merge_kernel(x_ref, gpre_ref, gpost_ref, of_ref, os_ref, od_ref, om_ref,
                  wgpt_ref, wmgt_ref, wbr_ref, wout_ref, y_ref):
    x = x_ref[...]
    hb = _rms(x, gpre_ref[...]).astype(BF16)
    y = jnp.zeros(x.shape, F32)
    for n, o_ref in enumerate((of_ref, os_ref, od_ref, om_ref)):
        gp = _nt(hb, wgpt_ref[BRANCH_W * n:BRANCH_W * (n + 1), :])
        o = o_ref[...] * (gp * _sigmoid(gp))
        u = _mm(o.astype(BF16), wbr_ref[n])
        mg = _nt(hb, wmgt_ref[D_MODEL * n:D_MODEL * (n + 1), :])
        y = y + _sigmoid(mg) * u
    out = _mm(y.astype(BF16), wout_ref[...])
    y_ref[...] = x + _rms(out, gpost_ref[...])


def _merge_call(x2, gpre, gpost, outs, wgpt, wmgt, wbr, wout, tm):
    T = x2.shape[0]
    row = lambda n: pl.BlockSpec((tm, n), lambda i: (i, 0))

    def const(a):
        return pl.BlockSpec(a.shape, lambda i: (0,) * a.ndim, pipeline_mode=pl.Buffered(1))

    return pl.pallas_call(
        _merge_kernel, grid=(T // tm,),
        in_specs=[row(D_MODEL), const(gpre), const(gpost)] + [row(BRANCH_W)] * 4
        + [const(wgpt), const(wmgt), const(wbr), const(wout)],
        out_specs=row(D_MODEL),
        out_shape=jax.ShapeDtypeStruct((T, D_MODEL), F32),
        compiler_params=_cparams(("parallel",), VMEM_LIMIT),
    )(x2, gpre, gpost, *outs, wgpt, wmgt, wbr, wout)


def _suffix_terms(x, run, pps):
    R = 8 * pps
    tri = (_iota((LANES, LANES), 0) > _iota((LANES, LANES), 1)).astype(BF16)
    ones = jnp.ones((LANES, LANES), BF16)
    r0, r1 = _iota((R, R), 0), _iota((R, R), 1)
    newer = ((r1 % 8 == r0 % 8) & (r1 < r0)).astype(BF16)
    parts = _split_bf16(x, 3)
    within = sum(_mm(p, tri) for p in parts)
    total = sum(_mm(p, ones) for p in parts)
    cross = sum(_mm(newer, p) for p in _split_bf16(total, 3))
    run_rows = jnp.concatenate([run] * pps, axis=0)
    last = slice(R - 8, R)
    new_run = run + cross[last, 0:1] + total[last, 0:1]
    return within + cross + run_rows, new_run


def _group_max(x, pps):
    return functools.reduce(jnp.maximum, [x[8 * i:8 * (i + 1)] for i in range(pps)])


def _group_sum(x, pps):
    return functools.reduce(jnp.add, [x[8 * i:8 * (i + 1)] for i in range(pps)])


def _decode_kernel(pt_ref, *refs, kind, pps, lam_init):
    del pt_ref
    it = iter(refs)
    q_ref, kvnew_ref = next(it), next(it)
    pages = [next(it) for _ in range(pps)]
    if kind == "fox":
        lfnew_ref = next(it)
        lfs = [next(it) for _ in range(pps)]
    if kind == "diff":
        dbias_ref, lam_ref, gsub_ref = next(it), next(it), next(it)
    o_ref = next(it)
    m_ref, l_ref, acc_ref, run_ref = next(it), next(it), next(it), next(it)
    g = pl.program_id(1)
    q = q_ref[...]

    @pl.when(g == 0)
    def _():
        if kind == "sb":
            run_ref[...] = jnp.zeros_like(run_ref)
            acc_ref[...] = jnp.zeros_like(acc_ref)
        else:
            knew = kvnew_ref[:, 0:256].astype(BF16).astype(F32)
            vnew = kvnew_ref[:, 256:512].astype(BF16).astype(F32)
            s = jnp.sum(q.astype(F32) * knew, axis=-1, keepdims=True)
            if kind == "diff":
                s = s + dbias_ref[2][:, 0:1]
            m_ref[...] = s
            l_ref[...] = jnp.ones_like(l_ref)
            acc_ref[...] = jnp.broadcast_to(vnew, acc_ref.shape)
            if kind == "fox":
                run_ref[...] = lfnew_ref[...]

    if kind == "diff":
        def keys(page, c):
            return page[pl.ds(c, PAGE, stride=4), :].astype(BF16)
        s = jnp.concatenate(
            [_nt(q[:, 0:LANES], keys(p, 0)) + _nt(q[:, LANES:2 * LANES], keys(p, 1)) for p in pages],
            axis=0)

        def pv(w, page):
            return jnp.concatenate([_mm(w, keys(page, 2)), _mm(w, keys(page, 3))], axis=1)
    else:
        s = jnp.concatenate([_mm(q, p[0:256, :].astype(BF16)) for p in pages], axis=0)

        def pv(w, page):
            return _nt(w, page[256:512, :].astype(BF16))

    def weighted_values(w):
        wb = w.astype(BF16)
        return sum(pv(wb[8 * i:8 * (i + 1)], pages[i]) for i in range(pps))

    if kind == "sb":
        sp = _softplus(s)
        rev, new_run = _suffix_terms(-sp, run_ref[...], pps)
        acc_ref[...] = acc_ref[...] + weighted_values(jnp.exp((s - sp) + rev))
        run_ref[...] = new_run
    else:
        if kind == "fox":
            decay, new_run = _suffix_terms(
                jnp.concatenate([lf[...] for lf in lfs], axis=0), run_ref[...], pps)
            s = s + decay
            run_ref[...] = new_run
        else:
            first = jnp.where(g == 0, dbias_ref[0], dbias_ref[1])
            s = s + jnp.concatenate([first] + [dbias_ref[1]] * (pps - 1), axis=0)
        m = m_ref[...]
        m_new = jnp.maximum(m, _group_max(jnp.max(s, axis=-1, keepdims=True), pps))
        alpha = jnp.exp(m - m_new)
        p = jnp.exp(s - jnp.concatenate([m_new] * pps, axis=0))
        l_ref[...] = alpha * l_ref[...] + _group_sum(jnp.sum(p, axis=-1, keepdims=True), pps)
        acc_ref[...] = alpha * acc_ref[...] + weighted_values(p)
        m_ref[...] = m_new

    @pl.when(g == pl.num_programs(1) - 1)
    def _():
        if kind == "diff":
            n = acc_ref[...] / l_ref[...]
            lam = _diff_lambda(lam_ref, lam_init)
            o = n[0:DIFF_HEADS] - lam * n[DIFF_HEADS:2 * DIFF_HEADS]
            row = _iota((DIFF_HEADS, LANES), 0)
            o = jnp.where(row // 2 == 0, o[:, 0:LANES], o[:, LANES:2 * LANES])
            o_ref[...] = _rms(o, gsub_ref[...]) * (1.0 - lam_init)
        else:
            o = acc_ref[...] if kind == "sb" else acc_ref[...] / l_ref[...]
            row = _iota((FOX_HEADS, HD), 0)
            res = o[:, 0:HD]
            for kvh in range(1, 4):
                res = jnp.where(row // 2 == kvh, o[:, HD * kvh:HD * (kvh + 1)], res)
            o_ref[...] = res


def _decode_call(kind, layer, pt, qbd, kvnew, pool, extra, lam_init, pps):
    B, n_pages = pt.shape
    steps = n_pages // pps
    oshape = (DIFF_HEADS, LANES) if kind == "diff" else (FOX_HEADS, HD)

    def page_map(i):
        return lambda b, g, pt: (layer, pt[b, n_pages - 1 - (g * pps + i)], 0, 0)

    per_b = lambda a: pl.BlockSpec((None,) + a.shape[1:], lambda b, g, pt: (b,) + (0,) * (a.ndim - 1))
    const = lambda a: pl.BlockSpec(a.shape, lambda b, g, pt: (0,) * a.ndim)
    in_specs = [per_b(qbd), per_b(kvnew)]
    in_specs += [pl.BlockSpec((None, None, 512, LANES), page_map(i)) for i in range(pps)]
    args = [qbd, kvnew] + [pool] * pps
    if kind == "fox":
        lfnew, lfpool = extra
        in_specs += [per_b(lfnew)]
        in_specs += [pl.BlockSpec((None, None, FOX_HEADS, PAGE), page_map(i)) for i in range(pps)]
        args += [lfnew] + [lfpool] * pps
    elif kind == "diff":
        in_specs += [const(a) for a in extra]
        args += list(extra)
    grid_spec = pltpu.PrefetchScalarGridSpec(
        num_scalar_prefetch=1, grid=(B, steps), in_specs=in_specs,
        out_specs=pl.BlockSpec((None,) + oshape, lambda b, g, pt: (b, 0, 0)),
        scratch_shapes=[pltpu.VMEM((8, 1), F32), pltpu.VMEM((8, 1), F32),
                        pltpu.VMEM((8, 256), F32), pltpu.VMEM((8, 1), F32)])
    return pl.pallas_call(
        functools.partial(_decode_kernel, kind=kind, pps=pps, lam_init=lam_init),
        grid_spec=grid_spec,
        out_shape=jax.ShapeDtypeStruct((B,) + oshape, F32),
        compiler_params=_cparams(("parallel", "arbitrary"), VMEM_LIMIT),
    )(pt, *args)


def _memdec_kernel(q_ref, kv_ref, o_ref):
    n = kv_ref.shape[0] // (2 * MEM_HEADS)

    def rows(c):
        return kv_ref[pl.ds(c, n, stride=2 * MEM_HEADS), :].astype(BF16)

    q = q_ref[...]
    s = sum(_nt(q[:, MEM_HD * h:MEM_HD * (h + 1)], rows(h)) for h in range(MEM_HEADS))
    s = s * MEM_HD ** -0.5
    e = jnp.exp(s - jnp.max(s, axis=-1, keepdims=True))
    p = (e / jnp.sum(e, axis=-1, keepdims=True)).astype(BF16)
    row = _iota((8, MEM_HD), 0)
    res = jnp.zeros((8, MEM_HD), F32)
    for h in range(MEM_HEADS):
        res = jnp.where(row == h, _mm(p, rows(MEM_HEADS + h)), res)
    o_ref[...] = res[0:MEM_HEADS]


def _memdec_call(layer, qbd, cache):
    B = qbd.shape[0]
    return pl.pallas_call(
        _memdec_kernel, grid=(B,),
        in_specs=[pl.BlockSpec((None, 8, 512), lambda b: (b, 0, 0)),
                  pl.BlockSpec((None, None) + cache.shape[2:], lambda b: (layer, b, 0, 0))],
        out_specs=pl.BlockSpec((None, MEM_HEADS, MEM_HD), lambda b: (b, 0, 0)),
        out_shape=jax.ShapeDtypeStruct((B, MEM_HEADS, MEM_HD), F32),
        compiler_params=_cparams(("parallel",), VMEM_LIMIT),
    )(qbd, cache)


def _pick(n, pref):
    t = min(n, pref)
    while n % t:
        t //= 2
    return t


def _layer_weights(w_in_l, b_f_l, w_branch_l, w_out_l):
    c = (0, 512, 768, 1024, 1032, 1544, 1800, 2056, 2568, 2824, 3080, 3592, 5640, 9736)
    wt = jnp.transpose(w_in_l)
    return dict(
        wt=jnp.concatenate([wt[c[0]:c[3]], wt[c[4]:c[11]]], axis=0).astype(BF16),
        wft=wt[c[3]:c[4]].astype(BF16), bft=b_f_l.reshape(FOX_HEADS, 1),
        wgpt=wt[c[11]:c[12]].astype(BF16), wmgt=wt[c[12]:c[13]].astype(BF16),
        wbr=w_branch_l.astype(BF16), wout=w_out_l.astype(BF16))


def kernel(x_prompt, x_sample, cache_fox_kv, cache_fox_logf, cache_sb_kv, cache_diff_kv, cache_mem_kv,
           page_table, mem_prompt, rel_bias, g_pre, g_post, g_mem, w_in, b_f, w_mem_kv, lam,
           g_diff_sub, w_branch, w_out):
    B, S, D = x_prompt.shape
    DB = x_sample.shape[0]
    depth = w_in.shape[0]
    n_pool = cache_fox_kv.shape[1]
    n_pages = page_table.shape[1]
    mem_len = mem_prompt.shape[1]
    T = B * S
    t_att = _pick(S, 512)
    t_fox = _pick(S, 512)
    t_sb = _pick(S, 512)
    tm_proj = _pick(S, 512)
    tm_merge = _pick(T, 512)
    pps = _pick(n_pages, 64)

    feat_major = lambda c: jnp.transpose(c, (0, 1, 3, 4, 5, 2)).reshape(depth, n_pool, 512, PAGE)
    fox_pool = feat_major(cache_fox_kv)
    sb_pool = feat_major(cache_sb_kv)
    diff_pool = cache_diff_kv.reshape(depth, n_pool, 4 * PAGE, LANES)
    lf_pool = jnp.swapaxes(cache_fox_logf, 2, 3)
    mem_cache = cache_mem_kv.reshape(depth, DB, 2 * MEM_HEADS * mem_len, MEM_HD)
    bias_tiles, dec_bias = _bias_call(rel_bias, t_att)

    h8 = jnp.arange(8)
    sel_kv = (jnp.arange(4)[None, :] == (h8 // 2)[:, None]).astype(BF16)
    r8m, r8h = h8 // 4, h8 % 4
    sel_diff = ((jnp.arange(2)[None, :, None] == (r8h // 2)[:, None, None])
                & (jnp.arange(2)[None, None, :] == r8m[:, None, None])).astype(BF16)
    sel_mem = (jnp.arange(8)[:, None] == jnp.arange(4)[None, :]).astype(BF16)

    xp = x_prompt.reshape(T, D)
    xs = x_sample.reshape(DB, D)
    mem2 = mem_prompt.reshape(B * mem_len, D)
    st_p = [[] for _ in range(5)]
    st_s = [[] for _ in range(4)]
    for l in range(depth):
        lam_init = 0.8 - 0.6 * math.exp(-0.3 * l)
        w = _layer_weights(w_in[l], b_f[l], w_branch[l], w_out[l])
        gpre, gpost = g_pre[l].reshape(1, D), g_post[l].reshape(1, D)
        gsub = g_diff_sub[l].reshape(1, 2 * HD)

        (qft, ktfb, ktf, lft, qst, ktsb, kts, qdt, kdb, vdtb, kvdi, qm) = _proj_call(
            xp, gpre, w["wt"], w["wft"], w["bft"], B, S, tm_proj)
        mkvi, mkvb = _memproj_call(mem2, g_mem[l].reshape(1, D), w_mem_kv[l].astype(BF16),
                                   _pick(B * mem_len, 512))
        f_row, f_aug = _cumsum_call(lft)
        r3 = lambda a: a.reshape(B, S, a.shape[-1])
        o_fox = _attn_call(
            functools.partial(_fox_kernel, t=t_fox), [qft, ktfb, ktfb, f_aug, f_row],
            [_qt_spec(t_fox)] + _kt_specs(S) + [_batch_spec(f_aug), _batch_spec(f_row)], B, S, t_fox)
        o_sb = _attn_call(
            functools.partial(_sb_kernel, t=t_sb), [qst, ktsb, ktsb],
            [_qt_spec(t_sb)] + _kt_specs(S), B, S, t_sb)
        kd3 = r3(kdb)
        o_diff = _attn_call(
            functools.partial(_diff_kernel, t=t_att, lam_init=lam_init),
            [qdt, kd3, vdtb, bias_tiles, lam[l], gsub],
            [_qt_spec(t_att), _batch_spec(kd3), _batch_spec(vdtb),
             _const_spec(bias_tiles), _const_spec(lam[l]), _const_spec(gsub)], B, S, t_att)
        mkv3 = mkvb.reshape(B, mem_len, 1024)
        o_mem = _attn_call(_mem_kernel, [r3(qm), mkv3, mkv3],
                           [_q_spec(t_att, 512)] + _kv_specs(mem_len, 512), B, S, t_att)
        flat = lambda a: a.reshape(T, BRANCH_W)
        xp = _merge_call(xp, gpre, gpost, [flat(o_fox), flat(o_sb), flat(o_diff), flat(o_mem)],
                         w["wgpt"], w["wmgt"], w["wbr"], w["wout"], tm_merge)
        tok_major = lambda kt: jnp.transpose(kt.reshape(B, 2, 4, HD, S), (0, 4, 1, 2, 3))
        st_p[0].append(tok_major(ktf))
        st_p[1].append(jnp.swapaxes(lft, 1, 2))
        st_p[2].append(tok_major(kts))
        st_p[3].append(kvdi.reshape(B, S, 2, 2, 2 * HD))
        st_p[4].append(mkvi.reshape(B, mem_len, 2, MEM_HEADS, MEM_HD))

        (qft, _, ktf, lft, qst, _, kts, qdt, _, _, kvdi, qm) = _proj_call(
            xs, gpre, w["wt"], w["wft"], w["bft"], 1, DB, DB)
        kvf, kvs = jnp.transpose(ktf[0]), jnp.transpose(kts[0])
        kvd = kvdi.reshape(DB, 512)
        lfs = jnp.transpose(lft[0])
        qf, qs, qd = jnp.transpose(qft[0]), jnp.transpose(qst[0]), jnp.transpose(qdt[0])
        place = lambda q: (q.reshape(DB, 8, 1, HD) * sel_kv[None, :, :, None]).reshape(DB, 8, 256)
        qd_rows = jnp.tile(qd.reshape(DB, 1, DIFF_HEADS, 1, 2, HD), (1, 2, 1, 1, 1, 1)).reshape(DB, 8, 1, 2, HD)
        qbd_diff = (qd_rows * sel_diff[None, :, :, :, None]).reshape(DB, 8, 256)
        qbd_mem = (qm.reshape(DB, 1, MEM_HEADS, MEM_HD)
                   * sel_mem[None, :, :, None]).reshape(DB, 8, 512)
        new = lambda kv: kv.reshape(DB, 1, 512)
        a_fox = _decode_call("fox", l, page_table, place(qf), new(kvf), fox_pool,
                             (lfs.reshape(DB, FOX_HEADS, 1), lf_pool), lam_init, pps)
        a_sb = _decode_call("sb", l, page_table, place(qs), new(kvs), sb_pool, (), lam_init, pps)
        a_diff = _decode_call("diff", l, page_table, qbd_diff, new(kvd), diff_pool,
                              (dec_bias, lam[l], gsub), lam_init, pps)
        a_mem = _memdec_call(l, qbd_mem, mem_cache)
        flat_s = lambda a: a.reshape(DB, BRANCH_W)
        xs = _merge_call(xs, gpre, gpost, [flat_s(a_fox), flat_s(a_sb), flat_s(a_diff), flat_s(a_mem)],
                         w["wgpt"], w["wmgt"], w["wbr"], w["wout"], DB)
        st_s[0].append(kvf.reshape(DB, 1, 2, 4, HD))
        st_s[1].append(lfs.reshape(DB, 1, FOX_HEADS))
        st_s[2].append(kvs.reshape(DB, 1, 2, 4, HD))
        st_s[3].append(kvd.reshape(DB, 1, 2, 2, 2 * HD))

    return (xp.reshape(B, S, D), xs.reshape(DB, 1, D),
            *[jnp.stack(s, 0) for s in st_p], *[jnp.stack(s, 0) for s in st_s])
```

```python
import functools
import math

import jax
import jax.numpy as jnp
from jax import lax
from jax.experimental import pallas as pl
from jax.experimental.pallas import tpu as pltpu

F32 = jnp.float32
BF16 = jnp.bfloat16

D_MODEL = 1024
N_BRANCH = 4
BRANCH_W = D_MODEL // 2
FOX_HEADS = 8
HD = 64
DIFF_HEADS = 4
MEM_HEADS = 4
MEM_HD = 128
N_BUCKETS = 32
MAX_DISTANCE = 128
EPS = 1e-6
NEG = -1e30
PAGE = 128
LANES = 128
VMEM_LIMIT = 56 * 1024 * 1024
SB_DEAD = -104.0

_R_FQ, _R_FKV, _R_SQ, _R_SKV, _R_DQ, _R_DKV, _R_MQ = 0, 512, 1024, 1536, 2048, 2560, 3072


def _nt(a, b):
    return lax.dot_general(a, b, (((1,), (1,)), ((), ())), preferred_element_type=F32)


def _tn(a, b):
    return lax.dot_general(a, b, (((0,), (0,)), ((), ())), preferred_element_type=F32)


def _mm(a, b):
    return jnp.dot(a, b, preferred_element_type=F32)


def _split_bf16(x, parts):
    out = []
    r = x
    for i in range(parts):
        p = r.astype(BF16)
        out.append(p)
        if i + 1 < parts:
            r = r - p.astype(F32)
    return out


def _mm_split(x, w, parts=3):
    return sum(_mm(p, w) for p in _split_bf16(x, parts))


def _log_sigmoid(x):
    return jnp.minimum(x, 0.0) - jnp.log1p(jnp.exp(-jnp.abs(x)))


def _softplus(x):
    return jnp.maximum(x, 0.0) + jnp.log(1.0 + jnp.exp(-jnp.abs(x)))


def _sigmoid(x):
    return 1.0 / (1.0 + jnp.exp(-x))


def _rms(x, g):
    return x * lax.rsqrt(jnp.mean(x * x, axis=-1, keepdims=True) + EPS) * g


def _cparams(sem, vmem=None):
    return pltpu.CompilerParams(dimension_semantics=sem, vmem_limit_bytes=vmem)


def _iota(shape, dim):
    return lax.broadcasted_iota(jnp.int32, shape, dim)


def _proj_kernel(x_ref, g_ref, wt_ref, wft_ref, bft_ref,
                 qft_ref, ktfb_ref, ktf_ref, lft_ref, qst_ref, ktsb_ref, kts_ref,
                 qdt_ref, kdb_ref, vdtb_ref, kvdi_ref, qm_ref):
    hb = _rms(x_ref[...], g_ref[...]).astype(BF16)
    tm = hb.shape[0]

    def rows(r0, n=512):
        return wt_ref[r0:r0 + n, :]

    qft_ref[...] = (_nt(rows(_R_FQ), hb) * 0.125).astype(BF16)
    kt = _nt(rows(_R_FKV), hb)
    ktf_ref[...] = kt
    ktfb_ref[...] = kt.astype(BF16)
    qst_ref[...] = (_nt(rows(_R_SQ), hb) * 0.125).astype(BF16)
    kt = _nt(rows(_R_SKV), hb)
    kts_ref[...] = kt
    ktsb_ref[...] = kt.astype(BF16)
    qdt_ref[...] = (_nt(rows(_R_DQ), hb) * 0.125).astype(BF16)
    vdtb_ref[...] = _nt(rows(_R_DKV + 256, 256), hb).astype(BF16)
    kv = _nt(hb, rows(_R_DKV))
    kdb_ref[...] = kv[:, 0:256].astype(BF16)
    for c in range(4):
        kvdi_ref[pl.ds(c, tm, stride=4), :] = kv[:, LANES * c:LANES * (c + 1)]
    qm_ref[...] = _nt(hb, rows(_R_MQ)).astype(BF16)
    lft_ref[...] = _log_sigmoid(_nt(wft_ref[...], hb) + bft_ref[...])


def _proj_call(x2, g, wt, wft, bft, B, S, tm):
    T = B * S
    nt = S // tm
    row = lambda n: pl.BlockSpec((tm, n), lambda b, i: (b * nt + i, 0))
    col = lambda n: pl.BlockSpec((None, n, tm), lambda b, i: (b, 0, i))
    full = lambda a: pl.BlockSpec(a.shape, lambda b, i: (0,) * a.ndim)
    sds = jax.ShapeDtypeStruct
    out_shape = (
        sds((B, 512, S), BF16), sds((B, 512, S), BF16), sds((B, 512, S), F32),
        sds((B, FOX_HEADS, S), F32),
        sds((B, 512, S), BF16), sds((B, 512, S), BF16), sds((B, 512, S), F32),
        sds((B, 512, S), BF16), sds((T, 256), BF16), sds((B, 256, S), BF16),
        sds((4 * T, LANES), F32), sds((T, 512), BF16))
    out_specs = (
        col(512), col(512), col(512), col(FOX_HEADS),
        col(512), col(512), col(512),
        col(512), row(256), col(256),
        pl.BlockSpec((4 * tm, LANES), lambda b, i: (b * nt + i, 0)), row(512))
    return pl.pallas_call(
        _proj_kernel, grid=(B, nt),
        in_specs=[row(D_MODEL), full(g), full(wt), full(wft), full(bft)],
        out_specs=out_specs, out_shape=out_shape,
        compiler_params=_cparams(("parallel", "parallel"), VMEM_LIMIT),
    )(x2, g, wt, wft, bft)


def _memproj_kernel(x_ref, g_ref, w_ref, kvi_ref, kvb_ref):
    hb = _rms(x_ref[...], g_ref[...]).astype(BF16)
    tm = hb.shape[0]
    kv = _mm(hb, w_ref[...])
    kvb_ref[...] = kv.astype(BF16)
    for c in range(2 * MEM_HEADS):
        kvi_ref[pl.ds(c, tm, stride=2 * MEM_HEADS), :] = kv[:, MEM_HD * c:MEM_HD * (c + 1)]


def _memproj_call(x2, g, w, tm):
    T, N = x2.shape[0], w.shape[1]
    return pl.pallas_call(
        _memproj_kernel, grid=(T // tm,),
        in_specs=[pl.BlockSpec((tm, D_MODEL), lambda i: (i, 0)),
                  pl.BlockSpec(g.shape, lambda i: (0, 0)),
                  pl.BlockSpec(w.shape, lambda i: (0, 0))],
        out_specs=(pl.BlockSpec((2 * MEM_HEADS * tm, MEM_HD), lambda i: (i, 0)),
                   pl.BlockSpec((tm, N), lambda i: (i, 0))),
        out_shape=(jax.ShapeDtypeStruct((2 * MEM_HEADS * T, MEM_HD), F32),
                   jax.ShapeDtypeStruct((T, N), BF16)),
        compiler_params=_cparams(("parallel",), VMEM_LIMIT),
    )(x2, g, w)


AUG = 16


def _aug_rows(f2, width, lead):
    row = _iota((AUG, width), 0)
    out = jnp.zeros((AUG, width), F32)
    if lead is None:
        out = jnp.where((row >= 6) & (row < 9), 1.0, out)
        srcs = [(3 * g + i, p) for g in range(2) for i, p in enumerate(_split_bf16(f2[g:g + 1], 3))]
    else:
        out = jnp.where(row // 3 == lead, -1.0, out)
        srcs = [(6 + i, p) for i, p in enumerate(_split_bf16(f2[lead:lead + 1], 3))]
    for r, p in srcs:
        out = jnp.where(row == r, jnp.broadcast_to(p.astype(F32), (AUG, width)), out)
    return out.astype(BF16)


def _cumsum_kernel(lft_ref, f_ref, fa_ref, *, chunk):
    S = lft_ref.shape[1]
    tri = (_iota((chunk, chunk), 0) <= _iota((chunk, chunk), 1)).astype(BF16)
    carry = jnp.zeros((FOX_HEADS, 1), F32)
    for c in range(S // chunk):
        sl = slice(c * chunk, (c + 1) * chunk)
        loc = _mm_split(lft_ref[:, sl], tri)
        f = loc + carry
        f_ref[:, sl] = f
        for kvh in range(FOX_HEADS // 2):
            fa_ref[AUG * kvh:AUG * (kvh + 1), sl] = _aug_rows(f[2 * kvh:2 * kvh + 2], chunk, None)
        carry = carry + loc[:, chunk - 1:chunk]


def _cumsum_call(lft):
    B, _, S = lft.shape
    spec = pl.BlockSpec((None, FOX_HEADS, S), lambda b: (b, 0, 0))
    na = AUG * FOX_HEADS // 2
    return pl.pallas_call(
        functools.partial(_cumsum_kernel, chunk=min(256, S)), grid=(B,),
        in_specs=[spec], out_specs=(spec, pl.BlockSpec((None, na, S), lambda b: (b, 0, 0))),
        out_shape=(jax.ShapeDtypeStruct(lft.shape, F32), jax.ShapeDtypeStruct((B, na, S), BF16)),
        compiler_params=_cparams(("parallel",)),
    )(lft)


def _t5_bucket(dist):
    max_exact = N_BUCKETS // 2
    d1 = jnp.maximum(dist, 1).astype(F32)
    large = max_exact + (jnp.log(d1 / max_exact) / math.log(MAX_DISTANCE / max_exact)
                         * (N_BUCKETS - max_exact)).astype(jnp.int32)
    large = jnp.minimum(large, N_BUCKETS - 1)
    return jnp.where(dist < max_exact, dist, large)


def _bias_kernel(rb_ref, tiles_ref, dec_ref, *, tile):
    key = _iota((tile, tile), 0)
    qry = _iota((tile, tile), 1)
    for d in range(3):
        bkt = _t5_bucket(jnp.maximum(tile * d + qry - key, 0))
        for h in range(DIFF_HEADS):
            acc = jnp.zeros((tile, tile), F32)
            for b in range(N_BUCKETS):
                acc = jnp.where(bkt == b, rb_ref[b, h], acc)
            tiles_ref[d, h] = acc
    r = _iota((2 * DIFF_HEADS, LANES), 0) % DIFF_HEADS
    lane = _iota((2 * DIFF_HEADS, LANES), 1)
    for idx, dist in enumerate((PAGE - lane, jnp.full_like(lane, 2 * MAX_DISTANCE),
                                jnp.zeros_like(lane))):
        bkt = _t5_bucket(dist)
        acc = jnp.zeros((2 * DIFF_HEADS, LANES), F32)
        for h in range(DIFF_HEADS):
            for b in range(N_BUCKETS):
                acc = jnp.where((bkt == b) & (r == h), rb_ref[b, h], acc)
        dec_ref[idx] = acc


def _bias_call(rel_bias, tile):
    return pl.pallas_call(
        functools.partial(_bias_kernel, tile=tile),
        in_specs=[pl.BlockSpec(memory_space=pltpu.SMEM)],
        out_specs=(pl.BlockSpec(memory_space=pltpu.VMEM),) * 2,
        out_shape=(jax.ShapeDtypeStruct((3, DIFF_HEADS, tile, tile), F32),
                   jax.ShapeDtypeStruct((3, 2 * DIFF_HEADS, LANES), F32)),
    )(rel_bias)


def _causal_mask(t):
    return _iota((t, t), 0) <= _iota((t, t), 1)


def _softmax_update(s, vt, carry):
    m, l, acc = carry
    m_new = jnp.maximum(m, jnp.max(s, axis=0, keepdims=True))
    alpha = jnp.exp(m - m_new)
    p = jnp.exp(s - m_new)
    l = alpha * l + jnp.sum(p, axis=0, keepdims=True)
    acc = alpha * acc + _mm(vt, p.astype(BF16))
    return m_new, l, acc


def _softmax_init(t, rows):
    return (jnp.full((1, t), NEG, F32), jnp.zeros((1, t), F32), jnp.zeros((rows, t), F32))


def _store_heads(o_ref, outs):
    for j in range(len(outs) // 2):
        pair = jnp.concatenate([outs[2 * j], outs[2 * j + 1]], axis=0)
        o_ref[:, LANES * j:LANES * (j + 1)] = pair.T


def _fox_kernel(qt_ref, kt_ref, vt_ref, fa_ref, frow_ref, o_ref, *, t):
    qi = pl.program_id(1)
    q0 = pl.multiple_of(qi * t, t)
    mask = _causal_mask(t)
    qa = [jnp.concatenate(
        [qt_ref[HD * h:HD * (h + 1), :],
         _aug_rows(frow_ref[2 * (h // 2):2 * (h // 2) + 2, pl.ds(q0, t)], t, h % 2)], axis=0)
        for h in range(FOX_HEADS)]

    def step(kb, carry, masked):
        ks = pl.multiple_of(kb * t, t)
        out = []
        for h in range(FOX_HEADS):
            kvh = h // 2
            ka = jnp.concatenate([kt_ref[HD * kvh:HD * (kvh + 1), pl.ds(ks, t)],
                                  fa_ref[AUG * kvh:AUG * (kvh + 1), pl.ds(ks, t)]], axis=0)
            s = _tn(ka, qa[h])
            if masked:
                s = jnp.where(mask, s, NEG)
            out.append(_softmax_update(s, vt_ref[HD * kvh:HD * (kvh + 1), pl.ds(ks, t)], carry[h]))
        return tuple(out)

    init = tuple(_softmax_init(t, HD) for _ in range(FOX_HEADS))
    carry = lax.fori_loop(0, qi, functools.partial(step, masked=False), init)
    carry = step(qi, carry, True)
    _store_heads(o_ref, [acc / l for _, l, acc in carry])


def _sb_kernel(qt_ref, kt_ref, vt_ref, o_ref, *, t):
    qi = pl.program_id(1)
    strict = _iota((t, t), 0) < _iota((t, t), 1)
    sub = min(t // 2, 256)
    later = (_iota((sub, sub), 1) > _iota((sub, sub), 0)).astype(BF16)

    def step(kb, carry, masked, nq=t, nk=t):
        ks = pl.multiple_of(kb * nk, nk)
        out = []
        for h in range(FOX_HEADS):
            run, acc = carry[h]
            kvh = h // 2
            run_c = run[:, :nq]
            z = _tn(kt_ref[HD * kvh:HD * (kvh + 1), pl.ds(ks, nk)], qt_ref[HD * h:HD * (h + 1), 0:nq])
            sp = _softplus(z)
            l1m = -sp
            if masked:
                l1m = jnp.where(strict, l1m, 0.0)
            revs = []
            for j in reversed(range(nk // sub)):
                x = l1m[sub * j:sub * (j + 1)]
                r = sum(_mm(later, p) for p in _split_bf16(x, 2)) + run_c
                revs.insert(0, r)
                run_c = r[0:1, :] + x[0:1, :]
            w = jnp.exp((z - sp) + jnp.concatenate(revs, axis=0))
            if masked:
                w = jnp.where(strict, w, 0.0)
            acc_c = acc[:, :nq] + _mm(vt_ref[HD * kvh:HD * (kvh + 1), pl.ds(ks, nk)], w.astype(BF16))
            if nq < t:
                run_c = jnp.concatenate([run_c, run[:, nq:]], axis=1)
                acc_c = jnp.concatenate([acc_c, acc[:, nq:]], axis=1)
            out.append((run_c, acc_c))
        return tuple(out)

    def alive(carry, lo=0):
        top = functools.reduce(jnp.maximum, [c[0][:, lo:] for c in carry])
        return jnp.max(top) > SB_DEAD

    init = tuple((jnp.zeros((1, t), F32), jnp.zeros((HD, t), F32)) for _ in range(FOX_HEADS))
    carry = step(qi, init, True)

    def body(state):
        kb, _, carry = state
        carry = lax.cond(alive(carry, t // 2), lambda c: step(kb, c, False, t, t // 2),
                         lambda c: step(kb, c, False, t // 2, t // 2), carry)
        return kb - 1, alive(carry), carry

    _, _, carry = lax.while_loop(lambda st: jnp.logical_and(st[0] >= 0, st[1]), body,
                                 (2 * qi - 1, alive(carry), carry))
    _store_heads(o_ref, [acc for _, acc in carry])


def _diff_lambda(lam_ref, lam_init):
    lf = lam_ref[...]
    a = jnp.sum(lf[0:1] * lf[1:2], axis=-1, keepdims=True)
    b = jnp.sum(lf[2:3] * lf[3:4], axis=-1, keepdims=True)
    return jnp.exp(a) - jnp.exp(b) + lam_init


def _diff_kernel(qt_ref, k_ref, vt_ref, bias_ref, lam_ref, gsub_ref, o_ref, *, t, lam_init):
    qi = pl.program_id(1)
    mask = _causal_mask(t)
    row_hi = _iota((LANES, t), 0) >= HD
    qm = []
    for h in range(DIFF_HEADS):
        q = qt_ref[LANES * h:LANES * (h + 1), :]
        zero = jnp.zeros_like(q)
        qm += [jnp.where(row_hi, zero, q), jnp.where(row_hi, q, zero)]

    def step(kb, carry, masked):
        ks = pl.multiple_of(kb * t, t)
        d = jnp.minimum(qi - kb, 2)
        out = []
        for h in range(DIFF_HEADS):
            lo = LANES * (h // 2)
            k = k_ref[pl.ds(ks, t), lo:lo + LANES]
            vt = vt_ref[lo:lo + LANES, pl.ds(ks, t)]
            bias = bias_ref[d, h]
            for mp in range(2):
                s = _mm(k, qm[2 * h + mp]) + bias
                if masked:
                    s = jnp.where(mask, s, NEG)
                out.append(_softmax_update(s, vt, carry[2 * h + mp]))
        return tuple(out)

    init = tuple(_softmax_init(t, LANES) for _ in range(2 * DIFF_HEADS))
    carry = lax.fori_loop(0, qi, functools.partial(step, masked=False), init)
    carry = step(qi, carry, True)
    lam = _diff_lambda(lam_ref, lam_init)
    for h in range(DIFF_HEADS):
        (_, l0, a0), (_, l1, a1) = carry[2 * h], carry[2 * h + 1]
        o = (a0 / l0 - lam * (a1 / l1)).T
        o_ref[:, LANES * h:LANES * (h + 1)] = _rms(o, gsub_ref[...]) * (1.0 - lam_init)


def _mem_kernel(q_ref, k_ref, v_ref, o_ref):
    for h in range(MEM_HEADS):
        sl = slice(MEM_HD * h, MEM_HD * (h + 1))
        s = _nt(q_ref[:, sl], k_ref[:, sl]) * MEM_HD ** -0.5
        e = jnp.exp(s - jnp.max(s, axis=-1, keepdims=True))
        p = e / jnp.sum(e, axis=-1, keepdims=True)
        o_ref[:, sl] = _mm(p.astype(BF16), v_ref[:, sl])


def _attn_call(body, args, in_specs, B, S, t):
    return pl.pallas_call(
        body, grid=(B, S // t), in_specs=in_specs,
        out_specs=pl.BlockSpec((None, t, BRANCH_W), lambda b, i: (b, i, 0)),
        out_shape=jax.ShapeDtypeStruct((B, S, BRANCH_W), F32),
        compiler_params=_cparams(("parallel", "arbitrary"), VMEM_LIMIT),
    )(*args)


def _q_spec(t, w):
    return pl.BlockSpec((None, t, w), lambda b, i: (b, i, 0))


def _qt_spec(t):
    return pl.BlockSpec((None, 512, t), lambda b, i: (b, 0, i))


def _batch_spec(a):
    return pl.BlockSpec((None,) + a.shape[1:], lambda b, i: (b, 0, 0))


def _kt_specs(S):
    return [pl.BlockSpec((None, 256, S), lambda b, i: (b, 0, 0)),
            pl.BlockSpec((None, 256, S), lambda b, i: (b, 1, 0))]


def _kv_specs(n, w):
    return [pl.BlockSpec((None, n, w), lambda b, i: (b, 0, 0)),
            pl.BlockSpec((None, n, w), lambda b, i: (b, 0, 1))]


def _const_spec(a):
    return pl.BlockSpec(a.shape, lambda b, i: (0,) * a.ndim, pipeline_mode=pl.Buffered(1))


def _merge_kernel(x_ref, gpre_ref, gpost_ref, of_ref, os_ref, od_ref, om_ref,
                  wgpt_ref, wmgt_ref, wbr_ref, wout_ref, y_ref):
    x = x_ref[...]
    hb = _rms(x, gpre_ref[...]).astype(BF16)
    y = jnp.zeros(x.shape, F32)
    for n, o_ref in enumerate((of_ref, os_ref, od_ref, om_ref)):
        gp = _nt(hb, wgpt_ref[BRANCH_W * n:BRANCH_W * (n + 1), :])
        o = o_ref[...] * (gp * _sigmoid(gp))
        u = _mm(o.astype(BF16), wbr_ref[n])
        mg = _nt(hb, wmgt_ref[D_MODEL * n:D_MODEL * (n + 1), :])
        y = y + _sigmoid(mg) * u
    out = _mm(y.astype(BF16), wout_ref[...])
    y_ref[...] = x + _rms(out, gpost_ref[...])


def _merge_call(x2, gpre, gpost, outs, wgpt, wmgt, wbr, wout, tm):
    T = x2.shape[0]
    row = lambda n: pl.BlockSpec((tm, n), lambda i: (i, 0))

    def const(a):
        return pl.BlockSpec(a.shape, lambda i: (0,) * a.ndim, pipeline_mode=pl.Buffered(1))

    return pl.pallas_call(
        _merge_kernel, grid=(T // tm,),
        in_specs=[row(D_MODEL), const(gpre), const(gpost)] + [row(BRANCH_W)] * 4
        + [const(wgpt), const(wmgt), const(wbr), const(wout)],
        out_specs=row(D_MODEL),
        out_shape=jax.ShapeDtypeStruct((T, D_MODEL), F32),
        compiler_params=_cparams(("parallel",), VMEM_LIMIT),
    )(x2, gpre, gpost, *outs, wgpt, wmgt, wbr, wout)


def _suffix_terms(x, run, pps):
    R = 8 * pps
    tri = (_iota((LANES, LANES), 0) > _iota((LANES, LANES), 1)).astype(BF16)
    ones = jnp.ones((LANES, LANES), BF16)
    r0, r1 = _iota((R, R), 0), _iota((R, R), 1)
    newer = ((r1 % 8 == r0 % 8) & (r1 < r0)).astype(BF16)
    parts = _split_bf16(x, 2)
    within = sum(_mm(p, tri) for p in parts)
    total = sum(_mm(p, ones) for p in parts)
    cross = sum(_mm(newer, p) for p in _split_bf16(total, 2))
    run_rows = jnp.concatenate([run] * pps, axis=0)
    last = slice(R - 8, R)
    new_run = run + cross[last, 0:1] + total[last, 0:1]
    return within + cross + run_rows, new_run


def _group_max(x, pps):
    return functools.reduce(jnp.maximum, [x[8 * i:8 * (i + 1)] for i in range(pps)])


def _group_sum(x, pps):
    return functools.reduce(jnp.add, [x[8 * i:8 * (i + 1)] for i in range(pps)])


def _decode_kernel(pt_ref, *refs, kind, pps, lam_init):
    del pt_ref
    it = iter(refs)
    q_ref, kvnew_ref = next(it), next(it)
    pages = [next(it) for _ in range(pps)]
    if kind == "fox":
        lfnew_ref = next(it)
        lfs = [next(it) for _ in range(pps)]
    if kind == "diff":
        dbias_ref, lam_ref, gsub_ref = next(it), next(it), next(it)
    o_ref = next(it)
    m_ref, l_ref, acc_ref, run_ref = next(it), next(it), next(it), next(it)
    g = pl.program_id(1)
    q = q_ref[...]

    @pl.when(g == 0)
    def _():
        if kind == "sb":
            run_ref[...] = jnp.zeros_like(run_ref)
            acc_ref[...] = jnp.zeros_like(acc_ref)
        else:
            knew = kvnew_ref[:, 0:256].astype(BF16).astype(F32)
            vnew = kvnew_ref[:, 256:512].astype(BF16).astype(F32)
            s = jnp.sum(q.astype(F32) * knew, axis=-1, keepdims=True)
            if kind == "diff":
                s = s + dbias_ref[2][:, 0:1]
            m_ref[...] = s
            l_ref[...] = jnp.ones_like(l_ref)
            acc_ref[...] = jnp.broadcast_to(vnew, acc_ref.shape)
            if kind == "fox":
                run_ref[...] = lfnew_ref[...]

    if kind == "diff":
        def keys(page, c):
            return page[pl.ds(c, PAGE, stride=4), :].astype(BF16)
        s = jnp.concatenate(
            [_nt(q[:, 0:LANES], keys(p, 0)) + _nt(q[:, LANES:2 * LANES], keys(p, 1)) for p in pages],
            axis=0)

        def pv(w, page):
            return jnp.concatenate([_mm(w, keys(page, 2)), _mm(w, keys(page, 3))], axis=1)
    else:
        s = jnp.concatenate([_mm(q, p[0:256, :].astype(BF16)) for p in pages], axis=0)

        def pv(w, page):
            return _nt(w, page[256:512, :].astype(BF16))

    def weighted_values(w):
        wb = w.astype(BF16)
        return sum(pv(wb[8 * i:8 * (i + 1)], pages[i]) for i in range(pps))

    if kind == "sb":
        sp = _softplus(s)
        rev, new_run = _suffix_terms(-sp, run_ref[...], pps)
        acc_ref[...] = acc_ref[...] + weighted_values(jnp.exp((s - sp) + rev))
        run_ref[...] = new_run
    else:
        if kind == "fox":
            decay, new_run = _suffix_terms(
                jnp.concatenate([lf[...] for lf in lfs], axis=0), run_ref[...], pps)
            s = s + decay
            run_ref[...] = new_run
        else:
            first = jnp.where(g == 0, dbias_ref[0], dbias_ref[1])
            s = s + jnp.concatenate([first] + [dbias_ref[1]] * (pps - 1), axis=0)
        m = m_ref[...]
        m_new = jnp.maximum(m, _group_max(jnp.max(s, axis=-1, keepdims=True), pps))
        alpha = jnp.exp(m - m_new)
        p = jnp.exp(s - jnp.concatenate([m_new] * pps, axis=0))
        l_ref[...] = alpha * l_ref[...] + _group_sum(jnp.sum(p, axis=-1, keepdims=True), pps)
        acc_ref[...] = alpha * acc_ref[...] + weighted_values(p)
        m_ref[...] = m_new

    @pl.when(g == pl.num_programs(1) - 1)
    def _():
        if kind == "diff":
            n = acc_ref[...] / l_ref[...]
            lam = _diff_lambda(lam_ref, lam_init)
            o = n[0:DIFF_HEADS] - lam * n[DIFF_HEADS:2 * DIFF_HEADS]
            row = _iota((DIFF_HEADS, LANES), 0)
            o = jnp.where(row // 2 == 0, o[:, 0:LANES], o[:, LANES:2 * LANES])
            o_ref[...] = _rms(o, gsub_ref[...]) * (1.0 - lam_init)
        else:
            o = acc_ref[...] if kind == "sb" else acc_ref[...] / l_ref[...]
            row = _iota((FOX_HEADS, HD), 0)
            res = o[:, 0:HD]
            for kvh in range(1, 4):
                res = jnp.where(row // 2 == kvh, o[:, HD * kvh:HD * (kvh + 1)], res)
            o_ref[...] = res


def _decode_call(kind, layer, pt, qbd, kvnew, pool, extra, lam_init, pps):
    B, n_pages = pt.shape
    steps = n_pages // pps
    oshape = (DIFF_HEADS, LANES) if kind == "diff" else (FOX_HEADS, HD)

    def page_map(i):
        return lambda b, g, pt: (layer, pt[b, n_pages - 1 - (g * pps + i)], 0, 0)

    per_b = lambda a: pl.BlockSpec((None,) + a.shape[1:], lambda b, g, pt: (b,) + (0,) * (a.ndim - 1))
    const = lambda a: pl.BlockSpec(a.shape, lambda b, g, pt: (0,) * a.ndim)
    in_specs = [per_b(qbd), per_b(kvnew)]
    in_specs += [pl.BlockSpec((None, None, 512, LANES), page_map(i)) for i in range(pps)]
    args = [qbd, kvnew] + [pool] * pps
    if kind == "fox":
        lfnew, lfpool = extra
        in_specs += [per_b(lfnew)]
        in_specs += [pl.BlockSpec((None, None, FOX_HEADS, PAGE), page_map(i)) for i in range(pps)]
        args += [lfnew] + [lfpool] * pps
    elif kind == "diff":
        in_specs += [const(a) for a in extra]
        args += list(extra)
    grid_spec = pltpu.PrefetchScalarGridSpec(
        num_scalar_prefetch=1, grid=(B, steps), in_specs=in_specs,
        out_specs=pl.BlockSpec((None,) + oshape, lambda b, g, pt: (b, 0, 0)),
        scratch_shapes=[pltpu.VMEM((8, 1), F32), pltpu.VMEM((8, 1), F32),
                        pltpu.VMEM((8, 256), F32), pltpu.VMEM((8, 1), F32)])
    return pl.pallas_call(
        functools.partial(_decode_kernel, kind=kind, pps=pps, lam_init=lam_init),
        grid_spec=grid_spec,
        out_shape=jax.ShapeDtypeStruct((B,) + oshape, F32),
        compiler_params=_cparams(("parallel", "arbitrary"), VMEM_LIMIT),
    )(pt, *args)


def _memdec_kernel(q_ref, kv_ref, o_ref):
    n = kv_ref.shape[0] // (2 * MEM_HEADS)

    def rows(c):
        return kv_ref[pl.ds(c, n, stride=2 * MEM_HEADS), :].astype(BF16)

    q = q_ref[...]
    s = sum(_nt(q[:, MEM_HD * h:MEM_HD * (h + 1)], rows(h)) for h in range(MEM_HEADS))
    s = s * MEM_HD ** -0.5
    e = jnp.exp(s - jnp.max(s, axis=-1, keepdims=True))
    p = (e / jnp.sum(e, axis=-1, keepdims=True)).astype(BF16)
    row = _iota((8, MEM_HD), 0)
    res = jnp.zeros((8, MEM_HD), F32)
    for h in range(MEM_HEADS):
        res = jnp.where(row == h, _mm(p, rows(MEM_HEADS + h)), res)
    o_ref[...] = res[0:MEM_HEADS]


def _memdec_call(layer, qbd, cache):
    B = qbd.shape[0]
    return pl.pallas_call(
        _memdec_kernel, grid=(B,),
        in_specs=[pl.BlockSpec((None, 8, 512), lambda b: (b, 0, 0)),
                  pl.BlockSpec((None, None) + cache.shape[2:], lambda b: (layer, b, 0, 0))],
        out_specs=pl.BlockSpec((None, MEM_HEADS, MEM_HD), lambda b: (b, 0, 0)),
        out_shape=jax.ShapeDtypeStruct((B, MEM_HEADS, MEM_HD), F32),
        compiler_params=_cparams(("parallel",), VMEM_LIMIT),
    )(qbd, cache)


def _pick(n, pref):
    t = min(n, pref)
    while n % t:
        t //= 2
    return t


def _layer_weights(w_in_l, b_f_l, w_branch_l, w_out_l):
    c = (0, 512, 768, 1024, 1032, 1544, 1800, 2056, 2568, 2824, 3080, 3592, 5640, 9736)
    wt = jnp.transpose(w_in_l)
    return dict(
        wt=jnp.concatenate([wt[c[0]:c[3]], wt[c[4]:c[11]]], axis=0).astype(BF16),
        wft=wt[c[3]:c[4]].astype(BF16), bft=b_f_l.reshape(FOX_HEADS, 1),
        wgpt=wt[c[11]:c[12]].astype(BF16), wmgt=wt[c[12]:c[13]].astype(BF16),
        wbr=w_branch_l.astype(BF16), wout=w_out_l.astype(BF16))


def kernel(x_prompt, x_sample, cache_fox_kv, cache_fox_logf, cache_sb_kv, cache_diff_kv, cache_mem_kv,
           page_table, mem_prompt, rel_bias, g_pre, g_post, g_mem, w_in, b_f, w_mem_kv, lam,
           g_diff_sub, w_branch, w_out):
    B, S, D = x_prompt.shape
    DB = x_sample.shape[0]
    depth = w_in.shape[0]
    n_pool = cache_fox_kv.shape[1]
    n_pages = page_table.shape[1]
    mem_len = mem_prompt.shape[1]
    T = B * S
    t_att = _pick(S, 512)
    t_fox = _pick(S, 512)
    t_sb = _pick(S, 512)
    tm_proj = _pick(S, 512)
    tm_merge = _pick(T, 512)
    pps = _pick(n_pages, 64)

    feat_major = lambda c: jnp.transpose(c, (0, 1, 3, 4, 5, 2)).reshape(depth, n_pool, 512, PAGE)
    fox_pool = feat_major(cache_fox_kv)
    sb_pool = feat_major(cache_sb_kv)
    diff_pool = cache_diff_kv.reshape(depth, n_pool, 4 * PAGE, LANES)
    lf_pool = jnp.swapaxes(cache_fox_logf, 2, 3)
    mem_cache = cache_mem_kv.reshape(depth, DB, 2 * MEM_HEADS * mem_len, MEM_HD)
    bias_tiles, dec_bias = _bias_call(rel_bias, t_att)

    h8 = jnp.arange(8)
    sel_kv = (jnp.arange(4)[None, :] == (h8 // 2)[:, None]).astype(BF16)
    r8m, r8h = h8 // 4, h8 % 4
    sel_diff = ((jnp.arange(2)[None, :, None] == (r8h // 2)[:, None, None])
                & (jnp.arange(2)[None, None, :] == r8m[:, None, None])).astype(BF16)
    sel_mem = (jnp.arange(8)[:, None] == jnp.arange(4)[None, :]).astype(BF16)

    xp = x_prompt.reshape(T, D)
    xs = x_sample.reshape(DB, D)
    mem2 = mem_prompt.reshape(B * mem_len, D)
    st_p = [[] for _ in range(5)]
    st_s = [[] for _ in range(4)]
    for l in range(depth):
        lam_init = 0.8 - 0.6 * math.exp(-0.3 * l)
        w = _layer_weights(w_in[l], b_f[l], w_branch[l], w_out[l])
        gpre, gpost = g_pre[l].reshape(1, D), g_post[l].reshape(1, D)
        gsub = g_diff_sub[l].reshape(1, 2 * HD)

        (qft, ktfb, ktf, lft, qst, ktsb, kts, qdt, kdb, vdtb, kvdi, qm) = _proj_call(
            xp, gpre, w["wt"], w["wft"], w["bft"], B, S, tm_proj)
        mkvi, mkvb = _memproj_call(mem2, g_mem[l].reshape(1, D), w_mem_kv[l].astype(BF16),
                                   _pick(B * mem_len, 512))
        f_row, f_aug = _cumsum_call(lft)
        r3 = lambda a: a.reshape(B, S, a.shape[-1])
        o_fox = _attn_call(
            functools.partial(_fox_kernel, t=t_fox), [qft, ktfb, ktfb, f_aug, f_row],
            [_qt_spec(t_fox)] + _kt_specs(S) + [_batch_spec(f_aug), _batch_spec(f_row)], B, S, t_fox)
        o_sb = _attn_call(
            functools.partial(_sb_kernel, t=t_sb), [qst, ktsb, ktsb],
            [_qt_spec(t_sb)] + _kt_specs(S), B, S, t_sb)
        kd3 = r3(kdb)
        o_diff = _attn_call(
            functools.partial(_diff_kernel, t=t_att, lam_init=lam_init),
            [qdt, kd3, vdtb, bias_tiles, lam[l], gsub],
            [_qt_spec(t_att), _batch_spec(kd3), _batch_spec(vdtb),
             _const_spec(bias_tiles), _const_spec(lam[l]), _const_spec(gsub)], B, S, t_att)
        mkv3 = mkvb.reshape(B, mem_len, 1024)
        o_mem = _attn_call(_mem_kernel, [r3(qm), mkv3, mkv3],
                           [_q_spec(t_att, 512)] + _kv_specs(mem_len, 512), B, S, t_att)
        flat = lambda a: a.reshape(T, BRANCH_W)
        xp = _merge_call(xp, gpre, gpost, [flat(o_fox), flat(o_sb), flat(o_diff), flat(o_mem)],
                         w["wgpt"], w["wmgt"], w["wbr"], w["wout"], tm_merge)
        tok_major = lambda kt: jnp.transpose(kt.reshape(B, 2, 4, HD, S), (0, 4, 1, 2, 3))
        st_p[0].append(tok_major(ktf))
        st_p[1].append(jnp.swapaxes(lft, 1, 2))
        st_p[2].append(tok_major(kts))
        st_p[3].append(kvdi.reshape(B, S, 2, 2, 2 * HD))
        st_p[4].append(mkvi.reshape(B, mem_len, 2, MEM_HEADS, MEM_HD))

        (qft, _, ktf, lft, qst, _, kts, qdt, _, _, kvdi, qm) = _proj_call(
            xs, gpre, w["wt"], w["wft"], w["bft"], 1, DB, DB)
        kvf, kvs = jnp.transpose(ktf[0]), jnp.transpose(kts[0])
        kvd = kvdi.reshape(DB, 512)
        lfs = jnp.transpose(lft[0])
        qf, qs, qd = jnp.transpose(qft[0]), jnp.transpose(qst[0]), jnp.transpose(qdt[0])
        place = lambda q: (q.reshape(DB, 8, 1, HD) * sel_kv[None, :, :, None]).reshape(DB, 8, 256)
        qd_rows = jnp.tile(qd.reshape(DB, 1, DIFF_HEADS, 1, 2, HD), (1, 2, 1, 1, 1, 1)).reshape(DB, 8, 1, 2, HD)
        qbd_diff = (qd_rows * sel_diff[None, :, :, :, None]).reshape(DB, 8, 256)
        qbd_mem = (qm.reshape(DB, 1, MEM_HEADS, MEM_HD)
                   * sel_mem[None, :, :, None]).reshape(DB, 8, 512)
        new = lambda kv: kv.reshape(DB, 1, 512)
        a_fox = _decode_call("fox", l, page_table, place(qf), new(kvf), fox_pool,
                             (lfs.reshape(DB, FOX_HEADS, 1), lf_pool), lam_init, pps)
        a_sb = _decode_call("sb", l, page_table, place(qs), new(kvs), sb_pool, (), lam_init, pps)
        a_diff = _decode_call("diff", l, page_table, qbd_diff, new(kvd), diff_pool,
                              (dec_bias, lam[l], gsub), lam_init, pps)
        a_mem = _memdec_call(l, qbd_mem, mem_cache)
        flat_s = lambda a: a.reshape(DB, BRANCH_W)
        xs = _merge_call(xs, gpre, gpost, [flat_s(a_fox), flat_s(a_sb), flat_s(a_diff), flat_s(a_mem)],
                         w["wgpt"], w["wmgt"], w["wbr"], w["wout"], DB)
        st_s[0].append(kvf.reshape(DB, 1, 2, 4, HD))
        st_s[1].append(lfs.reshape(DB, 1, FOX_HEADS))
        st_s[2].append(kvs.reshape(DB, 1, 2, 4, HD))
        st_s[3].append(kvd.reshape(DB, 1, 2, 2, 2 * HD))

    return (xp.reshape(B, S, D), xs.reshape(DB, 1, D),
            *[jnp.stack(s, 0) for s in st_p], *[jnp.stack(s, 0) for s in st_s])
```
